```python
import jax, jax.numpy as jnp
from jax import lax
import numpy as np

D_MODEL = 2048
BATCH = 32
SEQ = 256
DEPTH = 2
DEC_BATCH = 8
DEC_SEQ = 2048
PAST_LEN = 256

GRID_W = 64
N_MIXERS = 2
N_NA_LAYERS = (DEPTH + 1) // 2
N_DN_LAYERS = DEPTH // 2
EPS = 1e-6
Q_BLOCK = 128
NA_HEADS = 16
NA_HEAD_DIM = D_MODEL // NA_HEADS
NA_WIDTH = NA_HEADS * NA_HEAD_DIM
NA_IN = 4 * NA_WIDTH
WIN_R = 8
WIN_C = 16
DN_K_HEADS = 16
DN_V_HEADS = 32
DN_HEAD_DIM = 128
DN_KEY_WIDTH = DN_K_HEADS * DN_HEAD_DIM
DN_VAL_WIDTH = DN_V_HEADS * DN_HEAD_DIM
DN_CONV_WIDTH = 2 * DN_KEY_WIDTH + DN_VAL_WIDTH
DN_IN = DN_CONV_WIDTH + DN_VAL_WIDTH + 4 * DN_V_HEADS
DN_CONV = 5
DN_CHUNK = 64

kernel_name = "hybrid_natten_deltanet_diffusion_step"


def rmsnorm(x, w):
    xf = x.astype(jnp.float32)
    y = xf * lax.rsqrt(jnp.mean(xf * xf, axis=-1, keepdims=True) + EPS)
    return (y * w.astype(jnp.float32)).astype(x.dtype)


def l2norm(x):
    xf = x.astype(jnp.float32)
    return xf * lax.rsqrt(jnp.sum(xf * xf, axis=-1, keepdims=True) + EPS)


def ada_mod(cond, w, b):
    m = (jax.nn.silu(cond) @ w + b)[..., None, :]
    return jnp.split(m, 3, axis=-1)


def na_project(h, w_in):
    B, T, _ = h.shape
    q, k, v, z = jnp.split(h @ w_in, 4, axis=-1)
    shp = (B, T, NA_HEADS, NA_HEAD_DIM)
    return q.reshape(shp), k.reshape(shp), v.reshape(shp), z


def na_output(o, z, w_out):
    B, T = o.shape[:2]
    return (o.reshape(B, T, NA_WIDTH) * jax.nn.silu(z)) @ w_out


def context_attention(q, k, v):
    B, T, H, Dh = q.shape
    qb = jnp.moveaxis(q.reshape(B, T // Q_BLOCK, Q_BLOCK, H, Dh), 1, 0)

    def block(qi):
        s = jnp.einsum('bqhd,bkhd->bhqk', qi, k).astype(jnp.float32) * (Dh ** -0.5)
        p = jax.nn.softmax(s, axis=-1).astype(v.dtype)
        return jnp.einsum('bhqk,bkhd->bqhd', p, v)

    o = lax.map(block, qb)
    return jnp.moveaxis(o, 0, 1).reshape(B, T, H, Dh)


def neighbourhood_attention(q, k, v, ck, cv, rpb):
    B, T, H, Dh = q.shape
    rows = T // GRID_W
    wr = min(WIN_R, rows)
    qg = q.reshape(B, rows, GRID_W, H, Dh)
    kg = k.reshape(B, rows, GRID_W, H, Dh)
    vg = v.reshape(B, rows, GRID_W, H, Dh)
    cols = jnp.arange(GRID_W)
    col_start = jnp.clip(cols - WIN_C // 2, 0, GRID_W - WIN_C)
    col_in = (cols[None, :] >= col_start[:, None]) & (cols[None, :] < col_start[:, None] + WIN_C)
    dc_idx = jnp.clip(cols[None, :] - cols[:, None] + WIN_C - 1, 0, 2 * WIN_C - 2)
    scale = Dh ** -0.5
    n_loc = wr * GRID_W

    def row_block(r):
        rs = jnp.clip(r - wr // 2, 0, rows - wr)
        kb = lax.dynamic_slice_in_dim(kg, rs, wr, axis=1)
        vb = lax.dynamic_slice_in_dim(vg, rs, wr, axis=1)
        qr = lax.dynamic_index_in_dim(qg, r, axis=1, keepdims=False)
        s_loc = jnp.einsum('bqhd,bikhd->bhqik', qr, kb).astype(jnp.float32) * scale
        dr_idx = rs + jnp.arange(wr) - r + WIN_R - 1
        bias = rpb[:, dr_idx[None, :, None], dc_idx[:, None, :]]
        s_loc = jnp.where(col_in[:, None, :], s_loc + bias.astype(jnp.float32)[None], -jnp.inf)
        s_ctx = jnp.einsum('bqhd,bkhd->bhqk', qr, ck).astype(jnp.float32) * scale
        s = jnp.concatenate([s_loc.reshape(B, H, GRID_W, n_loc), s_ctx], axis=-1)
        p = jax.nn.softmax(s, axis=-1).astype(v.dtype)
        p_loc = p[..., :n_loc].reshape(B, H, GRID_W, wr, GRID_W)
        o = (jnp.einsum('bhqik,bikhd->bqhd', p_loc, vb)
             + jnp.einsum('bhqk,bkhd->bqhd', p[..., n_loc:], cv))
        return o

    out = lax.map(row_block, jnp.arange(rows))
    return jnp.moveaxis(out, 0, 1).reshape(B, T, H, Dh)


def short_conv(x, w):
    C = x.shape[-1]
    return lax.conv_general_dilated(
        x, w.astype(x.dtype)[:, None, :], window_strides=(1,),
        padding=[(DN_CONV // 2, DN_CONV // 2)],
        dimension_numbers=('NWC', 'WIO', 'NWC'), feature_group_count=C)


def chunk_gated_delta(q, k, v, g, beta, s0):
    B, H, T, Dk = k.shape
    Dv = v.shape[-1]
    n = T // DN_CHUNK

    def chunks(a):
        return a.astype(jnp.float32).reshape(B, H, n, DN_CHUNK, *a.shape[3:])

    q, k, v, g, beta = chunks(q), chunks(k), chunks(v), chunks(g), chunks(beta)
    gc = jnp.cumsum(g, axis=-1)
    idx = jnp.arange(DN_CHUNK)
    incl = idx[:, None] >= idx[None, :]
    strict = idx[:, None] > idx[None, :]
    decay = jnp.exp(jnp.where(incl, gc[..., :, None] - gc[..., None, :], -jnp.inf))
    kb = k * beta[..., None]
    a_mat = (jnp.where(strict, jnp.einsum('bhnid,bhnjd->bhnij', kb, k) * decay, 0.0)
             + jnp.eye(DN_CHUNK, dtype=jnp.float32))
    rhs = jnp.concatenate([v * beta[..., None], kb * jnp.exp(gc)[..., None]], axis=-1)
    sol = lax.linalg.triangular_solve(a_mat, rhs, left_side=True, lower=True, unit_diagonal=True)
    u, w = sol[..., :Dv], sol[..., Dv:]
    qk = jnp.einsum('bhnid,bhnjd->bhnij', q, k) * decay
    xs = tuple(jnp.moveaxis(a, 2, 0) for a in (q, k, u, w, qk, gc))

    def step(s, inp):
        q_i, k_i, u_i, w_i, qk_i, g_i = inp
        v_new = u_i - jnp.einsum('bhcd,bhde->bhce', w_i, s)
        o_i = (jnp.einsum('bhcd,bhde->bhce', q_i * jnp.exp(g_i)[..., None], s)
               + jnp.einsum('bhij,bhje->bhie', qk_i, v_new))
        g_last = g_i[..., -1:]
        s = (s * jnp.exp(g_last)[..., None]
             + jnp.einsum('bhcd,bhce->bhde', k_i * jnp.exp(g_last - g_i)[..., None], v_new))
        return s, o_i

    s, o = lax.scan(step, s0.astype(jnp.float32), xs)
    return jnp.moveaxis(o, 0, 2).reshape(B, H, T, Dv), s


def deltanet_mix(h, w_in, conv_w, a_log, dt_bias, norm_w, w_out, s0_fwd, s0_bwd):
    B, T, _ = h.shape
    proj = h @ w_in
    qkv = jax.nn.silu(short_conv(proj[..., :DN_CONV_WIDTH], conv_w))
    z = proj[..., DN_CONV_WIDTH:DN_CONV_WIDTH + DN_VAL_WIDTH]
    ba = proj[..., DN_CONV_WIDTH + DN_VAL_WIDTH:].reshape(B, T, 4, DN_V_HEADS).astype(jnp.float32)
    rep = DN_V_HEADS // DN_K_HEADS
    q = qkv[..., :DN_KEY_WIDTH].reshape(B, T, DN_K_HEADS, DN_HEAD_DIM)
    k = qkv[..., DN_KEY_WIDTH:2 * DN_KEY_WIDTH].reshape(B, T, DN_K_HEADS, DN_HEAD_DIM)
    v = qkv[..., 2 * DN_KEY_WIDTH:].reshape(B, T, DN_V_HEADS, DN_HEAD_DIM)
    q = jnp.repeat(l2norm(q) * (DN_HEAD_DIM ** -0.5), rep, axis=2)
    k = jnp.repeat(l2norm(k), rep, axis=2)
    beta = jax.nn.sigmoid(ba[:, :, 0:2])
    g = -jnp.exp(a_log.astype(jnp.float32)) * jax.nn.softplus(ba[:, :, 2:4] + dt_bias.astype(jnp.float32))
    qh = jnp.transpose(q, (0, 2, 1, 3))
    kh = jnp.transpose(k, (0, 2, 1, 3))
    vh = jnp.transpose(v, (0, 2, 1, 3))
    gh = jnp.transpose(g, (0, 2, 3, 1))
    bh = jnp.transpose(beta, (0, 2, 3, 1))

    def rev(a):
        return jnp.flip(a, axis=2)

    o_f, s_f = chunk_gated_delta(qh, kh, vh, gh[:, 0], bh[:, 0], s0_fwd)
    o_b, s_b = chunk_gated_delta(rev(qh), rev(kh), rev(vh), rev(gh[:, 1]), rev(bh[:, 1]), s0_bwd)
    o = jnp.transpose(o_f + rev(o_b), (0, 2, 1, 3))
    o = rmsnorm(o, norm_w) * jax.nn.silu(z.reshape(B, T, DN_V_HEADS, DN_HEAD_DIM).astype(jnp.float32))
    out = o.reshape(B, T, DN_VAL_WIDTH).astype(h.dtype) @ w_out
    return out, jnp.stack([s_f, s_b], axis=1)


def context_trunk(x, c_ctx, norm_w, ada_w, ada_b, na_w_in, na_w_out,
                  dn_w_in, dn_conv_w, dn_a_log, dn_dt_bias, dn_norm_w, dn_w_out, final_norm_w):
    B = x.shape[0]
    ks, vs, ss = [], [], []
    for i in range(DEPTH):
        shift, scale, gate = ada_mod(c_ctx, ada_w[i], ada_b[i])
        h = rmsnorm(x, norm_w[i]) * (1 + scale) + shift
        j = i // N_MIXERS
        if i % N_MIXERS == 0:
            q, k, v, z = na_project(h, na_w_in[j])
            out = na_output(context_attention(q, k, v), z, na_w_out[j])
            ks.append(k)
            vs.append(v)
        else:
            zeros = jnp.zeros((B, DN_V_HEADS, DN_HEAD_DIM, DN_HEAD_DIM), jnp.float32)
            out, s = deltanet_mix(h, dn_w_in[j], dn_conv_w[j], dn_a_log[j], dn_dt_bias[j],
                                  dn_norm_w[j], dn_w_out[j], zeros, zeros)
            ss.append(s.astype(x.dtype))
        x = x + gate * out
    return rmsnorm(x, final_norm_w), jnp.stack(ks, axis=1), jnp.stack(vs, axis=1), jnp.stack(ss, axis=1)


def latent_trunk(x, c, cache_na_k, cache_na_v, state_dn, norm_w, ada_w, ada_b, na_w_in, na_w_out,
                 na_rpb, dn_w_in, dn_conv_w, dn_a_log, dn_dt_bias, dn_norm_w, dn_w_out, final_norm_w):
    for i in range(DEPTH):
        shift, scale, gate = ada_mod(c, ada_w[i], ada_b[i])
        h = rmsnorm(x, norm_w[i]) * (1 + scale) + shift
        j = i // N_MIXERS
        if i % N_MIXERS == 0:
            q, k, v, z = na_project(h, na_w_in[j])
            o = neighbourhood_attention(q, k, v, cache_na_k[:, j], cache_na_v[:, j], na_rpb[j])
            out = na_output(o, z, na_w_out[j])
        else:
            out, _ = deltanet_mix(h, dn_w_in[j], dn_conv_w[j], dn_a_log[j], dn_dt_bias[j],
                                  dn_norm_w[j], dn_w_out[j], state_dn[:, j, 0], state_dn[:, j, 1])
        x = x + gate * out
    return rmsnorm(x, final_norm_w)


def setup_inputs(seed: int = 0) -> dict:
    key = jax.random.key(seed)
    ks = jax.random.split(key, 24)
    f32 = jnp.float32

    def nrm(k, shape, s=1.0):
        return jax.random.normal(k, shape, f32) * s

    dt = jnp.exp(jax.random.uniform(ks[19], (N_DN_LAYERS, 2, DN_V_HEADS), f32,
                                    jnp.log(1e-3), jnp.log(1e-1)))
    return {
        "x_prompt": nrm(ks[0], (BATCH, SEQ, D_MODEL)),
        "x_sample": nrm(ks[1], (DEC_BATCH, DEC_SEQ, D_MODEL)),
        "cache_na_k": nrm(ks[2], (DEC_BATCH, N_NA_LAYERS, PAST_LEN, NA_HEADS, NA_HEAD_DIM)),
        "cache_na_v": nrm(ks[3], (DEC_BATCH, N_NA_LAYERS, PAST_LEN, NA_HEADS, NA_HEAD_DIM)),
        "state_dn": nrm(ks[4], (DEC_BATCH, N_DN_LAYERS, 2, DN_V_HEADS, DN_HEAD_DIM, DN_HEAD_DIM), 0.1),
        "c": nrm(ks[5], (DEC_BATCH, D_MODEL)),
        "c_ctx": nrm(ks[6], (D_MODEL,)),
        "norm_w": 1.0 + nrm(ks[7], (DEPTH, D_MODEL), 0.02),
        "ada_w": nrm(ks[8], (DEPTH, D_MODEL, 3 * D_MODEL), 0.5 * D_MODEL ** -0.5),
        "ada_b": nrm(ks[9], (DEPTH, 3 * D_MODEL), 0.02),
        "na_w_in": nrm(ks[10], (N_NA_LAYERS, D_MODEL, NA_IN), D_MODEL ** -0.5),
        "na_w_out": nrm(ks[11], (N_NA_LAYERS, NA_WIDTH, D_MODEL), NA_WIDTH ** -0.5),
        "na_rpb": nrm(ks[12], (N_NA_LAYERS, NA_HEADS, 2 * WIN_R - 1, 2 * WIN_C - 1), 0.1),
        "dn_w_in": nrm(ks[13], (N_DN_LAYERS, D_MODEL, DN_IN), D_MODEL ** -0.5),
        "dn_conv_w": nrm(ks[14], (N_DN_LAYERS, DN_CONV, DN_CONV_WIDTH), DN_CONV ** -0.5),
        "dn_a_log": jnp.log(jax.random.uniform(ks[15], (N_DN_LAYERS, 2, DN_V_HEADS), f32, 1.0, 16.0)),
        "dn_dt_bias": jnp.log(jnp.expm1(dt)),
        "dn_norm_w": 1.0 + nrm(ks[16], (N_DN_LAYERS, DN_HEAD_DIM), 0.02),
        "dn_w_out": nrm(ks[17], (N_DN_LAYERS, DN_VAL_WIDTH, D_MODEL), DN_VAL_WIDTH ** -0.5),
        "final_norm_w": 1.0 + nrm(ks[18], (D_MODEL,), 0.02),
    }


def reference(x_prompt, x_sample, cache_na_k, cache_na_v, state_dn, c, c_ctx, norm_w, ada_w, ada_b,
              na_w_in, na_w_out, na_rpb, dn_w_in, dn_conv_w, dn_a_log, dn_dt_bias, dn_norm_w,
              dn_w_out, final_norm_w):
    y_prompt, new_na_k, new_na_v, new_state_dn = context_trunk(
        x_prompt, c_ctx, norm_w, ada_w, ada_b, na_w_in, na_w_out,
        dn_w_in, dn_conv_w, dn_a_log, dn_dt_bias, dn_norm_w, dn_w_out, final_norm_w)
    y_sample = latent_trunk(
        x_sample, c, cache_na_k, cache_na_v, state_dn, norm_w, ada_w, ada_b, na_w_in, na_w_out,
        na_rpb, dn_w_in, dn_conv_w, dn_a_log, dn_dt_bias, dn_norm_w, dn_w_out, final_norm_w)
    return (y_prompt, y_sample, new_na_k, new_na_v, new_state_dn)
```

```python
import functools

import jax
import jax.numpy as jnp
from jax import lax
from jax.experimental import pallas as pl
from jax.experimental.pallas import tpu as pltpu

F32 = jnp.float32
BF16 = jnp.bfloat16

EPS = 1e-6
GRID_W = 64
NA_HEADS = 16
HEAD_DIM = 128
WIN_R = 8
WIN_C = 16
DN_K_HEADS = 16
DN_V_HEADS = 32
DN_CONV = 5
DN_CHUNK = 64

V7X_VMEM_BYTES = 64 * 1024 * 1024
VMEM_LIMIT = V7X_VMEM_BYTES - 8 * 1024 * 1024
SUBLANES = 8

NT_DIMS = (((1,), (1,)), ((), ()))
TN_DIMS = (((0,), (0,)), ((), ()))


def _params(*sem):
    return pltpu.CompilerParams(dimension_semantics=sem, vmem_limit_bytes=VMEM_LIMIT)


def _silu(x):
    return x * jax.nn.sigmoid(x)


def _split3(x):
    hi = x.astype(BF16)
    r1 = x - hi.astype(F32)
    mid = r1.astype(BF16)
    lo = (r1 - mid.astype(F32)).astype(BF16)
    return hi, mid, lo


def _div_pow2(x, n):
    assert n & (n - 1) == 0
    return jnp.right_shift(x, n.bit_length() - 1)


def _mm(a, b):
    return jnp.dot(a.astype(BF16), b.astype(BF16), preferred_element_type=F32)


def _ada_kernel(cond_ref, w_ref, b_ref, o_ref):
    a_hi, a_mid, a_lo = _split3(_silu(cond_ref[...]))
    w_hi, w_mid, w_lo = _split3(w_ref[0])
    dot = functools.partial(jnp.dot, preferred_element_type=F32)
    acc = dot(a_hi, w_hi) + (dot(a_hi, w_mid) + dot(a_mid, w_hi)) + (
        dot(a_hi, w_lo) + dot(a_mid, w_mid) + dot(a_lo, w_hi))
    o_ref[0] = acc + b_ref[0]


def ada_mod(cond, ada_w, ada_b, *, tn=768):
    depth, d, n = ada_w.shape
    r = cond.shape[0]
    return pl.pallas_call(
        _ada_kernel,
        grid=(depth, n // tn),
        in_specs=[pl.BlockSpec((r, d), lambda i, j: (0, 0)),
                  pl.BlockSpec((1, d, tn), lambda i, j: (i, 0, j)),
                  pl.BlockSpec((1, 1, tn), lambda i, j: (i, 0, j))],
        out_specs=pl.BlockSpec((1, r, tn), lambda i, j: (i, 0, j)),
        out_shape=jax.ShapeDtypeStruct((depth, r, n), F32),
        compiler_params=_params("parallel", "parallel"),
        name="ada_mod",
    )(cond, ada_w, ada_b.reshape(depth, 1, n))


def _norm_mm_kernel(x_ref, nw_ref, sh_ref, sc_ref, w_ref, *rest, ranges):
    out_refs, h_ref = rest[:-1], rest[-1]
    j = pl.program_id(1)

    @pl.when(j == 0)
    def _():
        x = x_ref[...]
        y = x * lax.rsqrt(jnp.mean(x * x, axis=-1, keepdims=True) + EPS) * nw_ref[...]
        h_ref[...] = (y * (1.0 + sc_ref[0]) + sh_ref[0]).astype(BF16)

    acc = jnp.dot(h_ref[...], w_ref[...], preferred_element_type=F32)
    for (start, ntiles), o_ref in zip(ranges, out_refs):
        @pl.when((j >= start) & (j < start + ntiles))
        def _(o_ref=o_ref):
            o_ref[...] = acc.astype(o_ref.dtype)


def norm_matmul(x, norm_w, shift, scale, w, outs, *, seq_len, tm=512, tn=1024):
    m, d = x.shape
    n = w.shape[1]
    nb = shift.shape[0]
    tn = min(tn, n)
    ranges, start = [], 0
    for width, _ in outs:
        ranges.append((start, width // tn))
        start += width // tn
    assert start * tn == n and m % tm == 0 and (nb == 1 or seq_len % tm == 0)

    def mod_map(i, j):
        return ((i * tm) // seq_len if nb > 1 else 0, 0, 0)

    def out_map(i, j, s, t):
        return (i, jnp.clip(j - s, 0, t - 1))

    return pl.pallas_call(
        functools.partial(_norm_mm_kernel, ranges=tuple(ranges)),
        grid=(m // tm, n // tn),
        in_specs=[pl.BlockSpec((tm, d), lambda i, j: (i, 0)),
                  pl.BlockSpec((1, d), lambda i, j: (0, 0)),
                  pl.BlockSpec((1, 1, d), mod_map),
                  pl.BlockSpec((1, 1, d), mod_map),
                  pl.BlockSpec((d, tn), lambda i, j: (0, j))],
        out_specs=[pl.BlockSpec((tm, tn), functools.partial(out_map, s=s, t=t)) for s, t in ranges],
        out_shape=[jax.ShapeDtypeStruct((m, width), dt) for width, dt in outs],
        scratch_shapes=[pltpu.VMEM((tm, d), BF16)],
        compiler_params=_params("parallel", "arbitrary"),
        name="norm_matmul",
    )(x, norm_w.reshape(1, d), shift, scale, w)


def _out_kernel(g_ref, w_ref, x_ref, gate_ref, fw_ref, o_ref, *, final):
    acc = jnp.dot(g_ref[...], w_ref[...], preferred_element_type=F32)
    xn = x_ref[...] + gate_ref[0] * acc
    if final:
        xn = xn * lax.rsqrt(jnp.mean(xn * xn, axis=-1, keepdims=True) + EPS) * fw_ref[...]
    o_ref[...] = xn


def out_residual(g, w, x, gate, final_w, *, seq_len, final, tm):
    m, k = g.shape
    d = w.shape[1]
    nb = gate.shape[0]
    assert m % tm == 0 and (nb == 1 or seq_len % tm == 0)

    def mod_map(i):
        return ((i * tm) // seq_len if nb > 1 else 0, 0, 0)

    return pl.pallas_call(
        functools.partial(_out_kernel, final=final),
        grid=(m // tm,),
        in_specs=[pl.BlockSpec((tm, k), lambda i: (i, 0)),
                  pl.BlockSpec((k, d), lambda i: (0, 0)),
                  pl.BlockSpec((tm, d), lambda i: (i, 0)),
                  pl.BlockSpec((1, 1, d), mod_map),
                  pl.BlockSpec((1, d), lambda i: (0, 0))],
        out_specs=pl.BlockSpec((tm, d), lambda i: (i, 0)),
        out_shape=jax.ShapeDtypeStruct((m, d), F32),
        compiler_params=_params("parallel"),
        name="out_residual",
    )(g, w, x, gate, final_w.reshape(1, d))


def _ctx_attn_kernel(q_ref, k_ref, v_ref, z_ref, o_ref, *, heads):
    scale = HEAD_DIM ** -0.5
    for h in range(heads):
        sl = slice(h * HEAD_DIM, (h + 1) * HEAD_DIM)
        q = q_ref[:, sl]
        k = k_ref[:, sl].astype(BF16)
        v = v_ref[:, sl].astype(BF16)
        s = lax.dot_general(q, k, NT_DIMS, preferred_element_type=F32) * scale
        p = jnp.exp(s - jnp.max(s, axis=-1, keepdims=True))
        o = jnp.dot(p.astype(BF16), v, preferred_element_type=F32) / jnp.sum(p, axis=-1, keepdims=True)
        o_ref[:, sl] = (o * _silu(z_ref[:, sl].astype(F32))).astype(BF16)


def context_attention(q, k, v, z, *, seq_len):
    m, width = q.shape
    spec = pl.BlockSpec((seq_len, width), lambda b: (b, 0))
    return pl.pallas_call(
        functools.partial(_ctx_attn_kernel, heads=width // HEAD_DIM),
        grid=(m // seq_len,),
        in_specs=[spec, spec, spec, spec],
        out_specs=spec,
        out_shape=jax.ShapeDtypeStruct((m, width), BF16),
        compiler_params=_params("parallel"),
        name="context_attention",
    )(q, k, v, z)


def _window_rows(rows):
    wr = min(WIN_R, rows)
    top = list(range(wr // 2 + 1))
    bottom = list(range(rows - wr + wr // 2 + 1, rows))
    return wr, top, bottom


def _bias_kernel(rpb_ref, o_ref, *, rows):
    h = pl.program_id(0)
    wr, top, bottom = _window_rows(rows)
    n_dr, n_dc = 2 * WIN_R - 1, 2 * WIN_C - 1
    lane = lax.broadcasted_iota(jnp.int32, (GRID_W, 2 * GRID_W), 1)
    qi = lax.broadcasted_iota(jnp.int32, (GRID_W, 2 * GRID_W), 0)
    kcol = lane & (GRID_W - 1)
    second = lane >= GRID_W
    dc = kcol - qi + (WIN_C - 1)
    col_start = jnp.clip(qi - WIN_C // 2, 0, GRID_W - WIN_C)
    col_in = (kcol >= col_start) & (kcol < col_start + WIN_C)

    pair_cache = {}

    def pair(d):
        if d not in pair_cache:
            acc = jnp.zeros((GRID_W, 2 * GRID_W), F32)
            for c in range(n_dc):
                lo = rpb_ref[(h * n_dr + d) * n_dc + c]
                hi = rpb_ref[(h * n_dr + d + 1) * n_dc + c]
                acc = jnp.where(dc == c, jnp.where(second, hi, lo), acc)
            pair_cache[d] = jnp.where(col_in, acc, -jnp.inf)
        return pair_cache[d]

    for t, r in enumerate(top + bottom):
        rs = min(max(r - wr // 2, 0), rows - wr)
        d0 = rs - r + WIN_R - 1
        for i in range(0, wr, 2):
            o_ref[0, t, :, i * GRID_W:(i + 2) * GRID_W] = pair(d0 + i)


def window_bias(rpb, *, rows):
    heads = rpb.shape[0]
    wr, top, bottom = _window_rows(rows)
    nt = len(top) + len(bottom)
    return pl.pallas_call(
        functools.partial(_bias_kernel, rows=rows),
        grid=(heads,),
        in_specs=[pl.BlockSpec(memory_space=pltpu.SMEM)],
        out_specs=pl.BlockSpec((1, nt, GRID_W, wr * GRID_W), lambda h: (h, 0, 0, 0)),
        out_shape=jax.ShapeDtypeStruct((heads, nt, GRID_W, wr * GRID_W), F32),
        compiler_params=_params("parallel"),
        name="window_bias",
    )(rpb.reshape(-1))


def _lat_attn_kernel(q_ref, k_ref, v_ref, z_ref, ck_ref, cv_ref, tbl_ref, o_ref, *, rows):
    scale = HEAD_DIM ** -0.5
    wr, top, _ = _window_rows(rows)
    n_top = len(top) - 1
    ck = ck_ref[0].astype(BF16)
    cv = cv_ref[0].astype(BF16)

    def row(r, carry):
        rs = jnp.clip(r - wr // 2, 0, rows - wr)
        t = jnp.minimum(r, n_top) + jnp.maximum(r - (rows - wr + wr // 2), 0)
        qrows = pl.ds(pl.multiple_of(r * GRID_W, GRID_W), GRID_W)
        krows = pl.ds(pl.multiple_of(rs * GRID_W, GRID_W), wr * GRID_W)
        q = q_ref[qrows, :]
        s_loc = lax.dot_general(q, k_ref[krows, :], NT_DIMS, preferred_element_type=F32) * scale + tbl_ref[0, t]
        s_ctx = lax.dot_general(q, ck, NT_DIMS, preferred_element_type=F32) * scale
        mx = jnp.maximum(jnp.max(s_loc, axis=-1, keepdims=True), jnp.max(s_ctx, axis=-1, keepdims=True))
        p_loc = jnp.exp(s_loc - mx)
        p_ctx = jnp.exp(s_ctx - mx)
        den = jnp.sum(p_loc, axis=-1, keepdims=True) + jnp.sum(p_ctx, axis=-1, keepdims=True)
        o = (jnp.dot(p_loc.astype(BF16), v_ref[krows, :], preferred_element_type=F32)
             + jnp.dot(p_ctx.astype(BF16), cv, preferred_element_type=F32)) / den
        o_ref[qrows, :] = (o * _silu(z_ref[qrows, :].astype(F32))).astype(BF16)
        return carry

    lax.fori_loop(0, rows, row, 0)


def neighbourhood_attention(q, k, v, z, ck, cv, bias, *, seq_len):
    m, width = q.shape
    heads = width // HEAD_DIM
    rows = seq_len // GRID_W
    ctx_len = ck.shape[1]
    tok = pl.BlockSpec((seq_len, HEAD_DIM), lambda h, b: (b, h))
    ctx = pl.BlockSpec((1, ctx_len, HEAD_DIM), lambda h, b: (b, 0, h))
    return pl.pallas_call(
        functools.partial(_lat_attn_kernel, rows=rows),
        grid=(heads, m // seq_len),
        in_specs=[tok, tok, tok, tok, ctx, ctx,
                  pl.BlockSpec((1,) + bias.shape[1:], lambda h, b: (h, 0, 0, 0))],
        out_specs=tok,
        out_shape=jax.ShapeDtypeStruct((m, width), BF16),
        compiler_params=_params("parallel", "parallel"),
        name="neighbourhood_attention",
    )(q, k, v, z, ck, cv, bias)


def _conv_kernel(x_ref, w_ref, o_ref, pad_ref, *, l2_scale, rows_per_step):
    t = x_ref.shape[1]
    tc = x_ref.shape[2]
    half = DN_CONV // 2
    pad_ref[0:SUBLANES, :] = jnp.zeros((SUBLANES, tc), F32)
    pad_ref[SUBLANES + t:2 * SUBLANES + t, :] = jnp.zeros((SUBLANES, tc), F32)
    pad_ref[SUBLANES:SUBLANES + t, :] = x_ref[0].astype(F32)

    def step(i, carry):
        r0 = pl.multiple_of(i * rows_per_step, rows_per_step)
        win = pad_ref[pl.ds(r0, rows_per_step + 2 * SUBLANES), :]
        first = SUBLANES - half
        acc = win[first:first + rows_per_step] * w_ref[0:1, :]
        for j in range(1, DN_CONV):
            acc = acc + win[first + j:first + j + rows_per_step] * w_ref[j:j + 1, :]
        y = _silu(acc)
        if l2_scale is not None:
            parts = []
            for g in range(tc // HEAD_DIM):
                yg = y[:, g * HEAD_DIM:(g + 1) * HEAD_DIM]
                parts.append(yg * (lax.rsqrt(jnp.sum(yg * yg, axis=-1, keepdims=True) + EPS) * l2_scale))
            y = jnp.concatenate(parts, axis=1)
        o_ref[0, pl.ds(r0, rows_per_step), :] = y.astype(BF16)
        return carry

    lax.fori_loop(0, t // rows_per_step, step, 0)


def conv_silu(x, conv_w, *, seq_len, col0, l2_scale, tc=512, rows_per_step=128):
    m, c = x.shape
    b = m // seq_len
    rows_per_step = min(rows_per_step, seq_len)
    out = pl.pallas_call(
        functools.partial(_conv_kernel, l2_scale=l2_scale, rows_per_step=rows_per_step),
        grid=(b, c // tc),
        in_specs=[pl.BlockSpec((1, seq_len, tc), lambda i, j: (i, 0, j)),
                  pl.BlockSpec((DN_CONV, tc), lambda i, j: (0, col0 // tc + j))],
        out_specs=pl.BlockSpec((1, seq_len, tc), lambda i, j: (i, 0, j)),
        out_shape=jax.ShapeDtypeStruct((b, seq_len, c), BF16),
        scratch_shapes=[pltpu.VMEM((seq_len + 2 * SUBLANES, tc), F32)],
        compiler_params=_params("parallel", "parallel"),
        name="conv_silu",
    )(x.reshape(b, seq_len, c), conv_w)
    return out.reshape(m, c)


def _gate_kernel(ba_ref, alog_ref, dtb_ref, o_ref):
    x = ba_ref[...]
    tm = x.shape[0]
    lane = lax.broadcasted_iota(jnp.int32, x.shape, 1)
    beta = jax.nn.sigmoid(x)
    xs = x + dtb_ref[...]
    softplus = jnp.maximum(xs, 0.0) + jnp.log1p(jnp.exp(-jnp.abs(xs)))
    g = jnp.where(lane >= 2 * DN_V_HEADS, -jnp.exp(alog_ref[...]) * softplus, 0.0)
    r = lax.broadcasted_iota(jnp.int32, (tm, tm), 0)
    c = lax.broadcasted_iota(jnp.int32, (tm, tm), 1)
    same = _div_pow2(r, DN_CHUNK) == _div_pow2(c, DN_CHUNK)
    lower = jnp.where(same & (c <= r), 1.0, 0.0).astype(BF16)
    upper = jnp.where(same & (c >= r), 1.0, 0.0).astype(BF16)
    dot = functools.partial(jnp.dot, preferred_element_type=F32)
    hi, mid, lo = _split3(g)
    cf = dot(lower, hi) + dot(lower, mid) + dot(lower, lo)
    cb = dot(upper, hi) + dot(upper, mid) + dot(upper, lo)
    gc = jnp.where(lane < 3 * DN_V_HEADS, cf, cb)
    o_ref[...] = jnp.where(lane < 2 * DN_V_HEADS, beta, gc)


def gate_sums(ba, a_log, dt_bias, *, tm=256):
    m, w = ba.shape
    zeros = jnp.zeros((2 * DN_V_HEADS,), F32)
    alog = jnp.concatenate([zeros, a_log.reshape(-1)]).reshape(1, w)
    dtb = jnp.concatenate([zeros, dt_bias.reshape(-1)]).reshape(1, w)
    return pl.pallas_call(
        _gate_kernel,
        grid=(m // tm,),
        in_specs=[pl.BlockSpec((tm, w), lambda i: (i, 0)),
                  pl.BlockSpec((1, w), lambda i: (0, 0)),
                  pl.BlockSpec((1, w), lambda i: (0, 0))],
        out_specs=pl.BlockSpec((tm, w), lambda i: (i, 0)),
        out_shape=jax.ShapeDtypeStruct((m, w), F32),
        compiler_params=_params("parallel"),
        name="gate_sums",
    )(ba, alog, dtb)


def _unit_tri_inverse(a):
    n = a.shape[0]
    r = lax.broadcasted_iota(jnp.int32, (n, n), 0)
    c = lax.broadcasted_iota(jnp.int32, (n, n), 1)

    def same_block(b):
        return _div_pow2(r, b) == _div_pow2(c, b)

    base = 8
    d = jnp.where(same_block(base), a, 0.0)
    d2 = _mm(d, d)
    d4 = _mm(d2, d2)
    t = jnp.where(r == c, 1.0, 0.0) - d
    t = t + _mm(t, d2)
    t = t + _mm(t, d4)
    b = base
    while b < n:
        e = jnp.where(same_block(2 * b) & jnp.logical_not(same_block(b)), a, 0.0)
        t = t - _mm(_mm(t, e), t)
        b *= 2
    return t


def _delta_kernel(*refs, n_chunks, has_init, emit_state):
    q_ref, k_ref, v_ref, z_ref, pc_ref, pr_ref, nw_ref = refs[:7]
    refs = refs[7:]
    s0_ref = None
    if has_init:
        s0_ref, refs = refs[0], refs[1:]
    o_ref, refs = refs[0], refs[1:]
    sn_ref = None
    if emit_state:
        sn_ref, refs = refs[0], refs[1:]
    acc_ref, s_ref = refs
    cs = DN_CHUNK
    row = lax.broadcasted_iota(jnp.int32, (cs, cs), 0)
    col = lax.broadcasted_iota(jnp.int32, (cs, cs), 1)

    acc_ref[...] = jnp.zeros(acc_ref.shape, F32)
    for d in range(2):
        for e in range(2):
            s_ref[2 * d + e] = s0_ref[0, d, e] if has_init else jnp.zeros((HEAD_DIM, HEAD_DIM), F32)

    def chunk(ci, carry):
        for d in range(2):
            c = ci if d == 0 else n_chunks - 1 - ci
            rows = pl.ds(pl.multiple_of(c * cs, cs), cs)
            qc = q_ref[rows, :]
            kc = k_ref[rows, :]
            kf = kc.astype(F32)
            pc = pc_ref[0, 0, rows, :]
            pr = pr_ref[0, 0, c]
            kk = lax.dot_general(kc, kc, NT_DIMS, preferred_element_type=F32)
            qk = lax.dot_general(qc, kc, NT_DIMS, preferred_element_type=F32)
            incl = (col <= row) if d == 0 else (col >= row)
            strict = (col < row) if d == 0 else (col > row)
            last = cs - 1 if d == 0 else 0
            for e in range(2):
                ib, ig = 2 * d + e, 4 + 2 * d + e
                beta = pc[:, ib:ib + 1]
                gcc = pc[:, ig:ig + 1]
                gcr = pr[ig:ig + 1, :]
                g_last = gcr[:, last:last + 1]
                decay = jnp.exp(jnp.where(incl, gcc - gcr, -jnp.inf))
                a = jnp.where(strict, beta * kk * decay, 0.0)
                t = _unit_tri_inverse(a)
                vc = v_ref[rows, e * HEAD_DIM:(e + 1) * HEAD_DIM].astype(F32)
                rhs = jnp.concatenate([vc * beta, kf * (beta * jnp.exp(gcc))], axis=1)
                sol = _mm(t, rhs)
                u, w = sol[:, :HEAD_DIM], sol[:, HEAD_DIM:]
                s = s_ref[2 * d + e]
                sb = s.astype(BF16)
                v_new = u - _mm(w, sb)
                o = _mm(qc.astype(F32) * jnp.exp(gcc), sb) + _mm(qk * decay, v_new)
                kd = (kf * jnp.exp(g_last - gcc)).astype(BF16)
                s_ref[2 * d + e] = s * jnp.exp(g_last) + lax.dot_general(
                    kd, v_new.astype(BF16), TN_DIMS, preferred_element_type=F32)
                acc_ref[rows, e * HEAD_DIM:(e + 1) * HEAD_DIM] += o
        return carry

    lax.fori_loop(0, n_chunks, chunk, 0)

    if emit_state:
        for d in range(2):
            for e in range(2):
                sn_ref[0, 0, d, e] = s_ref[2 * d + e]
    for e in range(2):
        sl = slice(e * HEAD_DIM, (e + 1) * HEAD_DIM)
        o = acc_ref[:, sl]
        y = o * lax.rsqrt(jnp.mean(o * o, axis=-1, keepdims=True) + EPS) * nw_ref[...]
        o_ref[:, sl] = (y * _silu(z_ref[:, sl].astype(F32))).astype(BF16)


def gated_delta(q, k, v, z, p, norm_w, s0, *, seq_len, emit_state):
    m = q.shape[0]
    b = m // seq_len
    n_chunks = seq_len // DN_CHUNK
    rep = DN_V_HEADS // DN_K_HEADS
    p5 = p.reshape(b, seq_len, 4, DN_K_HEADS, rep)
    pc = p5.transpose(0, 3, 1, 2, 4).reshape(b, DN_K_HEADS, seq_len, 4 * rep)
    pr = p5.reshape(b, n_chunks, DN_CHUNK, 4, DN_K_HEADS, rep).transpose(0, 4, 1, 3, 5, 2).reshape(
        b, DN_K_HEADS, n_chunks, 4 * rep, DN_CHUNK)
    d = HEAD_DIM
    in_specs = [pl.BlockSpec((seq_len, d), lambda i, h: (i, h)),
                pl.BlockSpec((seq_len, d), lambda i, h: (i, h)),
                pl.BlockSpec((seq_len, rep * d), lambda i, h: (i, h)),
                pl.BlockSpec((seq_len, rep * d), lambda i, h: (i, h)),
                pl.BlockSpec((1, 1, seq_len, 4 * rep), lambda i, h: (i, h, 0, 0)),
                pl.BlockSpec((1, 1, n_chunks, 4 * rep, DN_CHUNK), lambda i, h: (i, h, 0, 0, 0)),
                pl.BlockSpec((1, d), lambda i, h: (0, 0))]
    args = [q, k, v, z, pc, pr, norm_w.reshape(1, d)]
    if s0 is not None:
        in_specs.append(pl.BlockSpec((1, 2, rep, d, d), lambda i, h: (i, 0, h, 0, 0)))
        args.append(s0)
    out_specs = [pl.BlockSpec((seq_len, rep * d), lambda i, h: (i, h))]
    out_shape = [jax.ShapeDtypeStruct((m, DN_V_HEADS * d), BF16)]
    if emit_state:
        out_specs.append(pl.BlockSpec((1, 1, 2, rep, d, d), lambda i, h: (i, 0, 0, h, 0, 0)))
        out_shape.append(jax.ShapeDtypeStruct((b, 1, 2, DN_V_HEADS, d, d), F32))
    res = pl.pallas_call(
        functools.partial(_delta_kernel, n_chunks=n_chunks, has_init=s0 is not None, emit_state=emit_state),
        grid=(b, DN_K_HEADS),
        in_specs=in_specs,
        out_specs=out_specs,
        out_shape=out_shape,
        scratch_shapes=[pltpu.VMEM((seq_len, rep * d), F32), pltpu.VMEM((2 * rep, d, d), F32)],
        compiler_params=_params("parallel", "parallel"),
        name="gated_delta",
    )(*args)
    return res if emit_state else (res[0], None)


def _trunk(x, mod, rows, cache_k, cache_v, state, weights, *, emit):
    (norm_w, na_w_in, na_w_out, bias, dn_w_in, dn_w_ba, dn_conv_w, dn_a_log, dn_dt_bias,
     dn_norm_w, dn_w_out, final_norm_w) = weights
    b, t, d = x.shape
    x2 = x.reshape(b * t, d)
    kw = dict(seq_len=t)

    def mods(layer):
        mm = mod[layer, rows][:, None, :]
        return mm[..., :d], mm[..., d:2 * d], mm[..., 2 * d:]

    shift, scale, gate = mods(0)
    local = cache_k is not None
    kv_dt = BF16 if local else F32
    q, k, v, z = norm_matmul(x2, norm_w[0], shift, scale, na_w_in,
                             ((d, BF16), (d, kv_dt), (d, kv_dt), (d, BF16)), **kw)
    if local:
        g = neighbourhood_attention(q, k, v, z, cache_k, cache_v, bias, **kw)
    else:
        g = context_attention(q, k, v, z, **kw)
    x2 = out_residual(g, na_w_out, x2, gate, final_norm_w, final=False, tm=512, **kw)

    shift, scale, gate = mods(1)
    kw_dim, vw_dim = DN_K_HEADS * HEAD_DIM, DN_V_HEADS * HEAD_DIM
    qp, kp, vp, zz = norm_matmul(x2, norm_w[1], shift, scale, dn_w_in,
                                 ((kw_dim, BF16), (kw_dim, BF16), (vw_dim, BF16), (vw_dim, BF16)), **kw)
    (ba,) = norm_matmul(x2, norm_w[1], shift, scale, dn_w_ba, ((4 * DN_V_HEADS, F32),), **kw)
    qn = conv_silu(qp, dn_conv_w, col0=0, l2_scale=HEAD_DIM ** -0.5, **kw)
    kn = conv_silu(kp, dn_conv_w, col0=kw_dim, l2_scale=1.0, **kw)
    vn = conv_silu(vp, dn_conv_w, col0=2 * kw_dim, l2_scale=None, **kw)
    p = gate_sums(ba, dn_a_log, dn_dt_bias)
    og, s_new = gated_delta(qn, kn, vn, zz, p, dn_norm_w, state, emit_state=emit, **kw)
    y = out_residual(og, dn_w_out, x2, gate, final_norm_w, final=True, tm=256, **kw)
    return y.reshape(b, t, d), k, v, s_new


def kernel(x_prompt, x_sample, cache_na_k, cache_na_v, state_dn, c, c_ctx, norm_w, ada_w, ada_b, na_w_in, na_w_out, na_rpb, dn_w_in, dn_conv_w, dn_a_log, dn_dt_bias, dn_norm_w, dn_w_out, final_norm_w):
    bp, tp, d = x_prompt.shape
    bs, ts, _ = x_sample.shape
    ctx_len = cache_na_k.shape[2]
    assert norm_w.shape[0] == 2 and na_w_in.shape[0] == 1 and dn_w_in.shape[0] == 1

    n_cond = -(-(bs + 1) // SUBLANES) * SUBLANES
    cond = jnp.concatenate([c, c_ctx[None], jnp.zeros((n_cond - bs - 1, d), F32)])
    mod = ada_mod(cond, ada_w, ada_b)

    conv_width = 2 * DN_K_HEADS * HEAD_DIM + DN_V_HEADS * HEAD_DIM
    n_main = conv_width + DN_V_HEADS * HEAD_DIM
    weights = (norm_w, na_w_in[0].astype(BF16), na_w_out[0].astype(BF16),
               window_bias(na_rpb[0], rows=ts // GRID_W),
               dn_w_in[0, :, :n_main].astype(BF16), dn_w_in[0, :, n_main:].astype(BF16),
               dn_conv_w[0], dn_a_log[0], dn_dt_bias[0], dn_norm_w[0], dn_w_out[0].astype(BF16),
               final_norm_w)

    y_prompt, k_new, v_new, s_new = _trunk(
        x_prompt, mod, slice(bs, bs + 1), None, None, None, weights, emit=True)
    y_sample, _, _, _ = _trunk(
        x_sample, mod, slice(0, bs), cache_na_k[:, 0].reshape(bs, ctx_len, d),
        cache_na_v[:, 0].reshape(bs, ctx_len, d), state_dn[:, 0], weights, emit=False)

    kv_shape = (bp, 1, tp, NA_HEADS, HEAD_DIM)
    return (y_prompt, y_sample, k_new.reshape(kv_shape), v_new.reshape(kv_shape), s_new)
```

```python
import functools

import jax
import jax.numpy as jnp
from jax import lax
from jax.experimental import pallas as pl
from jax.experimental.pallas import tpu as pltpu

F32 = jnp.float32
BF16 = jnp.bfloat16

EPS = 1e-6
GRID_W = 64
NA_HEADS = 16
HEAD_DIM = 128
WIN_R = 8
WIN_C = 16
DN_K_HEADS = 16
DN_V_HEADS = 32
DN_CONV = 5
DN_CHUNK = 64

V7X_VMEM_BYTES = 64 * 1024 * 1024
VMEM_LIMIT = V7X_VMEM_BYTES - 8 * 1024 * 1024
SUBLANES = 8

NT_DIMS = (((1,), (1,)), ((), ()))
TN_DIMS = (((0,), (0,)), ((), ()))


def _params(*sem):
    return pltpu.CompilerParams(dimension_semantics=sem, vmem_limit_bytes=VMEM_LIMIT)


def _silu(x):
    return x * jax.nn.sigmoid(x)


def _split3(x):
    hi = x.astype(BF16)
    r1 = x - hi.astype(F32)
    mid = r1.astype(BF16)
    lo = (r1 - mid.astype(F32)).astype(BF16)
    return hi, mid, lo


def _div_pow2(x, n):
    assert n & (n - 1) == 0
    return jnp.right_shift(x, n.bit_length() - 1)


def _mm(a, b):
    return jnp.dot(a.astype(BF16), b.astype(BF16), preferred_element_type=F32)


def _ada_kernel(cond_ref, w_ref, b_ref, o_ref):
    a_hi, a_mid, a_lo = _split3(_silu(cond_ref[...]))
    w_hi, w_mid, w_lo = _split3(w_ref[0])
    dot = functools.partial(jnp.dot, preferred_element_type=F32)
    acc = dot(a_hi, w_hi) + (dot(a_hi, w_mid) + dot(a_mid, w_hi)) + (
        dot(a_hi, w_lo) + dot(a_mid, w_mid) + dot(a_lo, w_hi))
    o_ref[0] = acc + b_ref[0]


def ada_mod(cond, ada_w, ada_b, *, tn=768):
    depth, d, n = ada_w.shape
    r = cond.shape[0]
    return pl.pallas_call(
        _ada_kernel,
        grid=(depth, n // tn),
        in_specs=[pl.BlockSpec((r, d), lambda i, j: (0, 0)),
                  pl.BlockSpec((1, d, tn), lambda i, j: (i, 0, j)),
                  pl.BlockSpec((1, 1, tn), lambda i, j: (i, 0, j))],
        out_specs=pl.BlockSpec((1, r, tn), lambda i, j: (i, 0, j)),
        out_shape=jax.ShapeDtypeStruct((depth, r, n), F32),
        compiler_params=_params("parallel", "parallel"),
        name="ada_mod",
    )(cond, ada_w, ada_b.reshape(depth, 1, n))


def _norm_mm_kernel(x_ref, nw_ref, sh_ref, sc_ref, w_ref, *rest, ranges):
    out_refs, h_ref = rest[:-1], rest[-1]
    j = pl.program_id(1)

    @pl.when(j == 0)
    def _():
        x = x_ref[...]
        y = x * lax.rsqrt(jnp.mean(x * x, axis=-1, keepdims=True) + EPS) * nw_ref[...]
        h_ref[...] = (y * (1.0 + sc_ref[0]) + sh_ref[0]).astype(BF16)

    acc = jnp.dot(h_ref[...], w_ref[...], preferred_element_type=F32)
    for (start, ntiles), o_ref in zip(ranges, out_refs):
        @pl.when((j >= start) & (j < start + ntiles))
        def _(o_ref=o_ref):
            o_ref[...] = acc.astype(o_ref.dtype)


def norm_matmul(x, norm_w, shift, scale, w, outs, *, seq_len, tm=512, tn=1024):
    m, d = x.shape
    n = w.shape[1]
    nb = shift.shape[0]
    tn = min(tn, n)
    ranges, start = [], 0
    for width, _ in outs:
        ranges.append((start, width // tn))
        start += width // tn
    assert start * tn == n and m % tm == 0 and (nb == 1 or seq_len % tm == 0)

    def mod_map(i, j):
        return ((i * tm) // seq_len if nb > 1 else 0, 0, 0)

    def out_map(i, j, s, t):
        return (i, jnp.clip(j - s, 0, t - 1))

    return pl.pallas_call(
        functools.partial(_norm_mm_kernel, ranges=tuple(ranges)),
        grid=(m // tm, n // tn),
        in_specs=[pl.BlockSpec((tm, d), lambda i, j: (i, 0)),
                  pl.BlockSpec((1, d), lambda i, j: (0, 0)),
                  pl.BlockSpec((1, 1, d), mod_map),
                  pl.BlockSpec((1, 1, d), mod_map),
                  pl.BlockSpec((d, tn), lambda i, j: (0, j))],
        out_specs=[pl.BlockSpec((tm, tn), functools.partial(out_map, s=s, t=t)) for s, t in ranges],
        out_shape=[jax.ShapeDtypeStruct((m, width), dt) for width, dt in outs],
        scratch_shapes=[pltpu.VMEM((tm, d), BF16)],
        compiler_params=_params("parallel", "arbitrary"),
        name="norm_matmul",
    )(x, norm_w.reshape(1, d), shift, scale, w)


def _out_kernel(g_ref, w_ref, x_ref, gate_ref, fw_ref, o_ref, *, final):
    acc = jnp.dot(g_ref[...], w_ref[...], preferred_element_type=F32)
    xn = x_ref[...] + gate_ref[0] * acc
    if final:
        xn = xn * lax.rsqrt(jnp.mean(xn * xn, axis=-1, keepdims=True) + EPS) * fw_ref[...]
    o_ref[...] = xn


def out_residual(g, w, x, gate, final_w, *, seq_len, final, tm):
    m, k = g.shape
    d = w.shape[1]
    nb = gate.shape[0]
    assert m % tm == 0 and (nb == 1 or seq_len % tm == 0)

    def mod_map(i):
        return ((i * tm) // seq_len if nb > 1 else 0, 0, 0)

    return pl.pallas_call(
        functools.partial(_out_kernel, final=final),
        grid=(m // tm,),
        in_specs=[pl.BlockSpec((tm, k), lambda i: (i, 0)),
                  pl.BlockSpec((k, d), lambda i: (0, 0)),
                  pl.BlockSpec((tm, d), lambda i: (i, 0)),
                  pl.BlockSpec((1, 1, d), mod_map),
                  pl.BlockSpec((1, d), lambda i: (0, 0))],
        out_specs=pl.BlockSpec((tm, d), lambda i: (i, 0)),
        out_shape=jax.ShapeDtypeStruct((m, d), F32),
        compiler_params=_params("parallel"),
        name="out_residual",
    )(g, w, x, gate, final_w.reshape(1, d))


def _ctx_attn_kernel(q_ref, k_ref, v_ref, z_ref, o_ref, *, heads):
    scale = HEAD_DIM ** -0.5
    for h in range(heads):
        sl = slice(h * HEAD_DIM, (h + 1) * HEAD_DIM)
        q = q_ref[:, sl]
        k = k_ref[:, sl].astype(BF16)
        v = v_ref[:, sl].astype(BF16)
        s = lax.dot_general(q, k, NT_DIMS, preferred_element_type=F32) * scale
        p = jnp.exp(s - jnp.max(s, axis=-1, keepdims=True))
        o = jnp.dot(p.astype(BF16), v, preferred_element_type=F32) / jnp.sum(p, axis=-1, keepdims=True)
        o_ref[:, sl] = (o * _silu(z_ref[:, sl].astype(F32))).astype(BF16)


def context_attention(q, k, v, z, *, seq_len):
    m, width = q.shape
    spec = pl.BlockSpec((seq_len, width), lambda b: (b, 0))
    return pl.pallas_call(
        functools.partial(_ctx_attn_kernel, heads=width // HEAD_DIM),
        grid=(m // seq_len,),
        in_specs=[spec, spec, spec, spec],
        out_specs=spec,
        out_shape=jax.ShapeDtypeStruct((m, width), BF16),
        compiler_params=_params("parallel"),
        name="context_attention",
    )(q, k, v, z)


def _window_rows(rows):
    wr = min(WIN_R, rows)
    top = list(range(wr // 2 + 1))
    bottom = list(range(rows - wr + wr // 2 + 1, rows))
    return wr, top, bottom


def _bias_kernel(rpb_ref, o_ref, *, rows):
    h = pl.program_id(0)
    wr, top, bottom = _window_rows(rows)
    n_dr, n_dc = 2 * WIN_R - 1, 2 * WIN_C - 1
    lane = lax.broadcasted_iota(jnp.int32, (GRID_W, 2 * GRID_W), 1)
    qi = lax.broadcasted_iota(jnp.int32, (GRID_W, 2 * GRID_W), 0)
    kcol = lane & (GRID_W - 1)
    second = lane >= GRID_W
    dc = kcol - qi + (WIN_C - 1)
    col_start = jnp.clip(qi - WIN_C // 2, 0, GRID_W - WIN_C)
    col_in = (kcol >= col_start) & (kcol < col_start + WIN_C)

    pair_cache = {}

    def pair(d):
        if d not in pair_cache:
            acc = jnp.zeros((GRID_W, 2 * GRID_W), F32)
            for c in range(n_dc):
                lo = rpb_ref[(h * n_dr + d) * n_dc + c]
                hi = rpb_ref[(h * n_dr + d + 1) * n_dc + c]
                acc = jnp.where(dc == c, jnp.where(second, hi, lo), acc)
            pair_cache[d] = jnp.where(col_in, acc, -jnp.inf)
        return pair_cache[d]

    for t, r in enumerate(top + bottom):
        rs = min(max(r - wr // 2, 0), rows - wr)
        d0 = rs - r + WIN_R - 1
        for i in range(0, wr, 2):
            o_ref[0, t, :, i * GRID_W:(i + 2) * GRID_W] = pair(d0 + i)


def window_bias(rpb, *, rows):
    heads = rpb.shape[0]
    wr, top, bottom = _window_rows(rows)
    nt = len(top) + len(bottom)
    return pl.pallas_call(
        functools.partial(_bias_kernel, rows=rows),
        grid=(heads,),
        in_specs=[pl.BlockSpec(memory_space=pltpu.SMEM)],
        out_specs=pl.BlockSpec((1, nt, GRID_W, wr * GRID_W), lambda h: (h, 0, 0, 0)),
        out_shape=jax.ShapeDtypeStruct((heads, nt, GRID_W, wr * GRID_W), F32),
        compiler_params=_params("parallel"),
        name="window_bias",
    )(rpb.reshape(-1))


def _lat_attn_kernel(q_ref, k_ref, v_ref, z_ref, ck_ref, cv_ref, tbl_ref, o_ref, *, rows):
    scale = HEAD_DIM ** -0.5
    wr, top, _ = _window_rows(rows)
    n_top = len(top) - 1
    ck = ck_ref[0].astype(BF16)
    cv = cv_ref[0].astype(BF16)

    def row(r, carry):
        rs = jnp.clip(r - wr // 2, 0, rows - wr)
        t = jnp.minimum(r, n_top) + jnp.maximum(r - (rows - wr + wr // 2), 0)
        qrows = pl.ds(pl.multiple_of(r * GRID_W, GRID_W), GRID_W)
        krows = pl.ds(pl.multiple_of(rs * GRID_W, GRID_W), wr * GRID_W)
        q = q_ref[qrows, :]
        s_loc = lax.dot_general(q, k_ref[krows, :], NT_DIMS, preferred_element_type=F32) * scale + tbl_ref[0, t]
        s_ctx = lax.dot_general(q, ck, NT_DIMS, preferred_element_type=F32) * scale
        mx = jnp.maximum(jnp.max(s_loc, axis=-1, keepdims=True), jnp.max(s_ctx, axis=-1, keepdims=True))
        p_loc = jnp.exp(s_loc - mx)
        p_ctx = jnp.exp(s_ctx - mx)
        den = jnp.sum(p_loc, axis=-1, keepdims=True) + jnp.sum(p_ctx, axis=-1, keepdims=True)
        o = (jnp.dot(p_loc.astype(BF16), v_ref[krows, :], preferred_element_type=F32)
             + jnp.dot(p_ctx.astype(BF16), cv, preferred_element_type=F32)) / den
        o_ref[qrows, :] = (o * _silu(z_ref[qrows, :].astype(F32))).astype(BF16)
        return carry

    lax.fori_loop(0, rows, row, 0)


def neighbourhood_attention(q, k, v, z, ck, cv, bias, *, seq_len):
    m, width = q.shape
    heads = width // HEAD_DIM
    rows = seq_len // GRID_W
    ctx_len = ck.shape[1]
    tok = pl.BlockSpec((seq_len, HEAD_DIM), lambda h, b: (b, h))
    ctx = pl.BlockSpec((1, ctx_len, HEAD_DIM), lambda h, b: (b, 0, h))
    return pl.pallas_call(
        functools.partial(_lat_attn_kernel, rows=rows),
        grid=(heads, m // seq_len),
        in_specs=[tok, tok, tok, tok, ctx, ctx,
                  pl.BlockSpec((1,) + bias.shape[1:], lambda h, b: (h, 0, 0, 0))],
        out_specs=tok,
        out_shape=jax.ShapeDtypeStruct((m, width), BF16),
        compiler_params=_params("parallel", "parallel"),
        name="neighbourhood_attention",
    )(q, k, v, z, ck, cv, bias)


def _conv_kernel(x_ref, w_ref, o_ref, pad_ref, *, l2_scale, rows_per_step):
    t = x_ref.shape[1]
    tc = x_ref.shape[2]
    half = DN_CONV // 2
    pad_ref[0:SUBLANES, :] = jnp.zeros((SUBLANES, tc), F32)
    pad_ref[SUBLANES + t:2 * SUBLANES + t, :] = jnp.zeros((SUBLANES, tc), F32)
    pad_ref[SUBLANES:SUBLANES + t, :] = x_ref[0].astype(F32)

    def step(i, carry):
        r0 = pl.multiple_of(i * rows_per_step, rows_per_step)
        win = pad_ref[pl.ds(r0, rows_per_step + 2 * SUBLANES), :]
        first = SUBLANES - half
        acc = win[first:first + rows_per_step] * w_ref[0:1, :]
        for j in range(1, DN_CONV):
            acc = acc + win[first + j:first + j + rows_per_step] * w_ref[j:j + 1, :]
        y = _silu(acc)
        if l2_scale is not None:
            parts = []
            for g in range(tc // HEAD_DIM):
                yg = y[:, g * HEAD_DIM:(g + 1) * HEAD_DIM]
                parts.append(yg * (lax.rsqrt(jnp.sum(yg * yg, axis=-1, keepdims=True) + EPS) * l2_scale))
            y = jnp.concatenate(parts, axis=1)
        o_ref[0, pl.ds(r0, rows_per_step), :] = y.astype(BF16)
        return carry

    lax.fori_loop(0, t // rows_per_step, step, 0)


def conv_silu(x, conv_w, *, seq_len, col0, l2_scale, tc=512, rows_per_step=128):
    m, c = x.shape
    b = m // seq_len
    rows_per_step = min(rows_per_step, seq_len)
    out = pl.pallas_call(
        functools.partial(_conv_kernel, l2_scale=l2_scale, rows_per_step=rows_per_step),
        grid=(b, c // tc),
        in_specs=[pl.BlockSpec((1, seq_len, tc), lambda i, j: (i, 0, j)),
                  pl.BlockSpec((DN_CONV, tc), lambda i, j: (0, col0 // tc + j))],
        out_specs=pl.BlockSpec((1, seq_len, tc), lambda i, j: (i, 0, j)),
        out_shape=jax.ShapeDtypeStruct((b, seq_len, c), BF16),
        scratch_shapes=[pltpu.VMEM((seq_len + 2 * SUBLANES, tc), F32)],
        compiler_params=_params("parallel", "parallel"),
        name="conv_silu",
    )(x.reshape(b, seq_len, c), conv_w)
    return out.reshape(m, c)


def _gate_kernel(ba_ref, alog_ref, dtb_ref, o_ref):
    x = ba_ref[...]
    tm = x.shape[0]
    lane = lax.broadcasted_iota(jnp.int32, x.shape, 1)
    beta = jax.nn.sigmoid(x)
    xs = x + dtb_ref[...]
    softplus = jnp.maximum(xs, 0.0) + jnp.log1p(jnp.exp(-jnp.abs(xs)))
    g = jnp.where(lane >= 2 * DN_V_HEADS, -jnp.exp(alog_ref[...]) * softplus, 0.0)
    r = lax.broadcasted_iota(jnp.int32, (tm, tm), 0)
    c = lax.broadcasted_iota(jnp.int32, (tm, tm), 1)
    same = _div_pow2(r, DN_CHUNK) == _div_pow2(c, DN_CHUNK)
    lower = jnp.where(same & (c <= r), 1.0, 0.0).astype(BF16)
    upper = jnp.where(same & (c >= r), 1.0, 0.0).astype(BF16)
    dot = functools.partial(jnp.dot, preferred_element_type=F32)
    hi, mid, lo = _split3(g)
    cf = dot(lower, hi) + dot(lower, mid) + dot(lower, lo)
    cb = dot(upper, hi) + dot(upper, mid) + dot(upper, lo)
    gc = jnp.where(lane < 3 * DN_V_HEADS, cf, cb)
    o_ref[...] = jnp.where(lane < 2 * DN_V_HEADS, beta, gc)


def gate_sums(ba, a_log, dt_bias, *, tm=256):
    m, w = ba.shape
    zeros = jnp.zeros((2 * DN_V_HEADS,), F32)
    alog = jnp.concatenate([zeros, a_log.reshape(-1)]).reshape(1, w)
    dtb = jnp.concatenate([zeros, dt_bias.reshape(-1)]).reshape(1, w)
    return pl.pallas_call(
        _gate_kernel,
        grid=(m // tm,),
        in_specs=[pl.BlockSpec((tm, w), lambda i: (i, 0)),
                  pl.BlockSpec((1, w), lambda i: (0, 0)),
                  pl.BlockSpec((1, w), lambda i: (0, 0))],
        out_specs=pl.BlockSpec((tm, w), lambda i: (i, 0)),
        out_shape=jax.ShapeDtypeStruct((m, w), F32),
        compiler_params=_params("parallel"),
        name="gate_sums",
    )(ba, alog, dtb)


N_PROB = 4
PREPARE_GROUP = 8


def _block_diag(x, mask):
    xb = x.astype(BF16)
    return jnp.where(mask, jnp.concatenate([xb] * N_PROB, axis=0), jnp.zeros((), BF16))


def _unit_tri_inverse(a_list, row, col, eye, bd_mask):
    cs = a_list[0].shape[0]

    def same_block(b):
        return _div_pow2(row, b) == _div_pow2(col, b)

    def mm(xs, ys_bd):
        return [jnp.dot(x.astype(BF16), y, preferred_element_type=F32) for x, y in zip(xs, ys_bd)]

    def bd(xs):
        return [_block_diag(x, bd_mask) for x in xs]

    base = 8
    d = [jnp.where(same_block(base), a, 0.0) for a in a_list]
    d2 = mm(d, bd(d))
    d2_bd = bd(d2)
    d4_bd = bd(mm(d2, d2_bd))
    t = [eye - x for x in d]
    t = [x + y for x, y in zip(t, mm(t, d2_bd))]
    t = [x + y for x, y in zip(t, mm(t, d4_bd))]
    b = base
    while b < cs:
        join = same_block(2 * b) & jnp.logical_not(same_block(b))
        e_bd = bd([jnp.where(join, a, 0.0) for a in a_list])
        t = [x - y for x, y in zip(t, mm(mm(t, e_bd), bd(t)))]
        b *= 2
    return t


def _delta_kernel(*refs, n_chunks, has_init, emit_state):
    q_ref, k_ref, v_ref, z_ref, pc_ref, pr_ref, nw_ref = refs[:7]
    refs = refs[7:]
    s0_ref = None
    if has_init:
        s0_ref, refs = refs[0], refs[1:]
    o_ref, refs = refs[0], refs[1:]
    sn_ref = None
    if emit_state:
        sn_ref, refs = refs[0], refs[1:]
    acc_ref, s_ref, n_ref, pm_ref = refs
    cs, hd = DN_CHUNK, HEAD_DIM
    wide = N_PROB * cs

    row = lax.broadcasted_iota(jnp.int32, (cs, wide), 0)
    lane_d = lax.broadcasted_iota(jnp.int32, (hd, wide), 1)
    lane = lax.broadcasted_iota(jnp.int32, (cs, wide), 1)
    col = lane & (cs - 1)
    fwd = lane < 2 * cs
    strict = (fwd & (col < row)) | (jnp.logical_not(fwd) & (col > row))
    incl = strict | (col == row)
    eye = jnp.where(col == row, 1.0, 0.0)
    first_half = (lax.broadcasted_iota(jnp.int32, (cs, 2 * cs), 1) < cs)
    bd_mask = (_div_pow2(lax.broadcasted_iota(jnp.int32, (wide, wide), 0), cs)
               == _div_pow2(lax.broadcasted_iota(jnp.int32, (wide, wide), 1), cs))

    def side_by_side(c0, c1, c2, c3):
        half = first_half[:c0.shape[0]]
        return jnp.concatenate([jnp.where(half, c0, c1), jnp.where(half, c2, c3)], axis=1)

    group = min(PREPARE_GROUP, n_chunks)
    each = range(group)

    def prepare(gi, carry):
        cidx = [gi * group + j for j in each]
        rows = [pl.ds(pl.multiple_of(c * cs, cs), cs) for c in cidx]
        qc = [q_ref[r, :] for r in rows]
        kc = [k_ref[r, :] for r in rows]
        kf = [x.astype(F32) for x in kc]
        pc = [pc_ref[0, 0, r, :] for r in rows]
        gcr = [pr_ref[0, 0, c] for c in cidx]
        beta = [[x[:, p:p + 1] for p in range(N_PROB)] for x in pc]
        gcc = [[x[:, N_PROB + p:N_PROB + p + 1] for p in range(N_PROB)] for x in pc]
        prod = [lax.dot_general(jnp.concatenate([qc[j], kc[j]], axis=0), jnp.concatenate([kc[j]] * N_PROB, axis=0),
                                NT_DIMS, preferred_element_type=F32) for j in each]
        decay = [jnp.exp(jnp.where(incl, side_by_side(*gcc[j]) - gcr[j], -jnp.inf)) for j in each]
        a = [jnp.where(strict, side_by_side(*beta[j]) * prod[j][cs:] * decay[j], 0.0) for j in each]
        t = _unit_tri_inverse(a, row, col, eye, bd_mask)
        rhs = []
        for j in each:
            parts = []
            for p in range(N_PROB):
                vf = v_ref[rows[j], (p % 2) * hd:(p % 2 + 1) * hd].astype(F32)
                parts.append(jnp.concatenate([vf * beta[j][p], kf[j] * (beta[j][p] * jnp.exp(gcc[j][p]))], axis=1))
            rhs.append(jnp.concatenate(parts, axis=0).astype(BF16))
        sol = [jnp.dot(_block_diag(t[j], bd_mask), rhs[j], preferred_element_type=F32).astype(BF16)
               for j in each]
        osol = [jnp.dot(_block_diag(prod[j][:cs] * decay[j], bd_mask), sol[j], preferred_element_type=F32)
                for j in each]
        ksol = []
        for j in each:
            g_last = [gcc[j][p][(cs - 1 if p < 2 else 0):(cs if p < 2 else 1), :] for p in range(N_PROB)]
            kdt = jnp.concatenate([kf[j]] * N_PROB, axis=0).T * jnp.exp(side_by_side(*g_last) - gcr[j])
            lhs = jnp.concatenate([jnp.where((lane_d >= p * cs) & (lane_d < (p + 1) * cs), kdt, 0.0)
                                   for p in range(N_PROB)], axis=0).astype(BF16)
            ksol.append(jnp.dot(lhs, sol[j], preferred_element_type=F32))
        for j in each:
            qf = qc[j].astype(F32)
            for p in range(N_PROB):
                kp = ksol[j][p * hd:(p + 1) * hd]
                op = osol[j][p * cs:(p + 1) * cs]
                n_ref[cidx[j], p] = kp[:, :hd]
                pm_ref[cidx[j], p, 0:hd] = (-kp[:, hd:]).astype(BF16)
                pm_ref[cidx[j], p, hd:hd + cs] = (qf * jnp.exp(gcc[j][p]) - op[:, hd:]).astype(BF16)
            for e in range(2):
                acc_ref[rows[j], e * hd:(e + 1) * hd] = (osol[j][e * cs:(e + 1) * cs, :hd]
                                                         + osol[j][(2 + e) * cs:(3 + e) * cs, :hd])
        return carry

    lax.fori_loop(0, n_chunks // group, prepare, 0)

    for p in range(N_PROB):
        s_ref[p] = s0_ref[0, p // 2, p % 2] if has_init else jnp.zeros((hd, hd), F32)

    def advance(ci, carry):
        for p in range(N_PROB):
            e = p % 2
            c = ci if p < 2 else n_chunks - 1 - ci
            last = cs - 1 if p < 2 else 0
            rows = pl.ds(pl.multiple_of(c * cs, cs), cs)
            g_last = pc_ref[0, 0, pl.ds(c * cs + last, 1), N_PROB + p:N_PROB + p + 1]
            s = s_ref[p]
            ws = jnp.dot(pm_ref[c, p], s.astype(BF16), preferred_element_type=F32)
            s_ref[p] = s * jnp.exp(g_last) + n_ref[c, p] + ws[:hd]
            acc_ref[rows, e * hd:(e + 1) * hd] += ws[hd:]
        return carry

    lax.fori_loop(0, n_chunks, advance, 0)

    if emit_state:
        for d in range(2):
            for e in range(2):
                sn_ref[0, 0, d, e] = s_ref[2 * d + e]
    for e in range(2):
        sl = slice(e * HEAD_DIM, (e + 1) * HEAD_DIM)
        o = acc_ref[:, sl]
        y = o * lax.rsqrt(jnp.mean(o * o, axis=-1, keepdims=True) + EPS) * nw_ref[...]
        o_ref[:, sl] = (y * _silu(z_ref[:, sl].astype(F32))).astype(BF16)


def gated_delta(q, k, v, z, p, norm_w, s0, *, seq_len, emit_state):
    m = q.shape[0]
    b = m // seq_len
    n_chunks = seq_len // DN_CHUNK
    rep = DN_V_HEADS // DN_K_HEADS
    assert 2 * rep == N_PROB
    p5 = p.reshape(b, seq_len, 2, 2, DN_K_HEADS, rep)
    pc = p5.transpose(0, 4, 1, 2, 3, 5).reshape(b, DN_K_HEADS, seq_len, 2 * N_PROB)
    pr = p5[:, :, 1].reshape(b, n_chunks, DN_CHUNK, 2, DN_K_HEADS, rep).transpose(0, 4, 1, 3, 5, 2).reshape(
        b, DN_K_HEADS, n_chunks, 1, N_PROB * DN_CHUNK)
    d = HEAD_DIM
    in_specs = [pl.BlockSpec((seq_len, d), lambda i, h: (i, h)),
                pl.BlockSpec((seq_len, d), lambda i, h: (i, h)),
                pl.BlockSpec((seq_len, rep * d), lambda i, h: (i, h)),
                pl.BlockSpec((seq_len, rep * d), lambda i, h: (i, h)),
                pl.BlockSpec((1, 1, seq_len, 2 * N_PROB), lambda i, h: (i, h, 0, 0)),
                pl.BlockSpec((1, 1, n_chunks, 1, N_PROB * DN_CHUNK), lambda i, h: (i, h, 0, 0, 0)),
                pl.BlockSpec((1, d), lambda i, h: (0, 0))]
    args = [q, k, v, z, pc, pr, norm_w.reshape(1, d)]
    if s0 is not None:
        in_specs.append(pl.BlockSpec((1, 2, rep, d, d), lambda i, h: (i, 0, h, 0, 0)))
        args.append(s0)
    out_specs = [pl.BlockSpec((seq_len, rep * d), lambda i, h: (i, h))]
    out_shape = [jax.ShapeDtypeStruct((m, DN_V_HEADS * d), BF16)]
    if emit_state:
        out_specs.append(pl.BlockSpec((1, 1, 2, rep, d, d), lambda i, h: (i, 0, 0, h, 0, 0)))
        out_shape.append(jax.ShapeDtypeStruct((b, 1, 2, DN_V_HEADS, d, d), F32))
    res = pl.pallas_call(
        functools.partial(_delta_kernel, n_chunks=n_chunks, has_init=s0 is not None, emit_state=emit_state),
        grid=(b, DN_K_HEADS),
        in_specs=in_specs,
        out_specs=out_specs,
        out_shape=out_shape,
        scratch_shapes=[pltpu.VMEM((seq_len, rep * d), F32),
                        pltpu.VMEM((N_PROB, d, d), F32),
                        pltpu.VMEM((n_chunks, N_PROB, d, d), F32),
                        pltpu.VMEM((n_chunks, N_PROB, d + DN_CHUNK, d), BF16)],
        compiler_params=_params("parallel", "parallel"),
        name="gated_delta",
    )(*args)
    return res if emit_state else (res[0], None)


def _trunk(x, mod, rows, cache_k, cache_v, state, weights, *, emit):
    (norm_w, na_w_in, na_w_out, bias, dn_w_in, dn_w_ba, dn_conv_w, dn_a_log, dn_dt_bias,
     dn_norm_w, dn_w_out, final_norm_w) = weights
    b, t, d = x.shape
    x2 = x.reshape(b * t, d)
    kw = dict(seq_len=t)

    def mods(layer):
        mm = mod[layer, rows][:, None, :]
        return mm[..., :d], mm[..., d:2 * d], mm[..., 2 * d:]

    shift, scale, gate = mods(0)
    local = cache_k is not None
    kv_dt = BF16 if local else F32
    q, k, v, z = norm_matmul(x2, norm_w[0], shift, scale, na_w_in,
                             ((d, BF16), (d, kv_dt), (d, kv_dt), (d, BF16)), **kw)
    if local:
        g = neighbourhood_attention(q, k, v, z, cache_k, cache_v, bias, **kw)
    else:
        g = context_attention(q, k, v, z, **kw)
    x2 = out_residual(g, na_w_out, x2, gate, final_norm_w, final=False, tm=512, **kw)

    shift, scale, gate = mods(1)
    kw_dim, vw_dim = DN_K_HEADS * HEAD_DIM, DN_V_HEADS * HEAD_DIM
    qp, kp, vp, zz = norm_matmul(x2, norm_w[1], shift, scale, dn_w_in,
                                 ((kw_dim, BF16), (kw_dim, BF16), (vw_dim, BF16), (vw_dim, BF16)), **kw)
    (ba,) = norm_matmul(x2, norm_w[1], shift, scale, dn_w_ba, ((4 * DN_V_HEADS, F32),), **kw)
    qn = conv_silu(qp, dn_conv_w, col0=0, l2_scale=HEAD_DIM ** -0.5, **kw)
    kn = conv_silu(kp, dn_conv_w, col0=kw_dim, l2_scale=1.0, **kw)
    vn = conv_silu(vp, dn_conv_w, col0=2 * kw_dim, l2_scale=None, **kw)
    p = gate_sums(ba, dn_a_log, dn_dt_bias)
    og, s_new = gated_delta(qn, kn, vn, zz, p, dn_norm_w, state, emit_state=emit, **kw)
    y = out_residual(og, dn_w_out, x2, gate, final_norm_w, final=True, tm=256, **kw)
    return y.reshape(b, t, d), k, v, s_new


def kernel(x_prompt, x_sample, cache_na_k, cache_na_v, state_dn, c, c_ctx, norm_w, ada_w, ada_b, na_w_in, na_w_out, na_rpb, dn_w_in, dn_conv_w, dn_a_log, dn_dt_bias, dn_norm_w, dn_w_out, final_norm_w):
    bp, tp, d = x_prompt.shape
    bs, ts, _ = x_sample.shape
    ctx_len = cache_na_k.shape[2]
    assert norm_w.shape[0] == 2 and na_w_in.shape[0] == 1 and dn_w_in.shape[0] == 1

    n_cond = -(-(bs + 1) // SUBLANES) * SUBLANES
    cond = jnp.concatenate([c, c_ctx[None], jnp.zeros((n_cond - bs - 1, d), F32)])
    mod = ada_mod(cond, ada_w, ada_b)

    conv_width = 2 * DN_K_HEADS * HEAD_DIM + DN_V_HEADS * HEAD_DIM
    n_main = conv_width + DN_V_HEADS * HEAD_DIM
    weights = (norm_w, na_w_in[0].astype(BF16), na_w_out[0].astype(BF16),
               window_bias(na_rpb[0], rows=ts // GRID_W),
               dn_w_in[0, :, :n_main].astype(BF16), dn_w_in[0, :, n_main:].astype(BF16),
               dn_conv_w[0], dn_a_log[0], dn_dt_bias[0], dn_norm_w[0], dn_w_out[0].astype(BF16),
               final_norm_w)

    y_prompt, k_new, v_new, s_new = _trunk(
        x_prompt, mod, slice(bs, bs + 1), None, None, None, weights, emit=True)
    y_sample, _, _, _ = _trunk(
        x_sample, mod, slice(0, bs), cache_na_k[:, 0].reshape(bs, ctx_len, d),
        cache_na_v[:, 0].reshape(bs, ctx_len, d), state_dn[:, 0], weights, emit=False)

    kv_shape = (bp, 1, tp, NA_HEADS, HEAD_DIM)
    return (y_prompt, y_sample, k_new.reshape(kv_shape), v_new.reshape(kv_shape), s_new)
```

```python
import functools

import jax
import jax.numpy as jnp
from jax import lax
from jax.experimental import pallas as pl
from jax.experimental.pallas import tpu as pltpu

F32 = jnp.float32
BF16 = jnp.bfloat16

EPS = 1e-6
GRID_W = 64
NA_HEADS = 16
HEAD_DIM = 128
WIN_R = 8
WIN_C = 16
DN_K_HEADS = 16
DN_V_HEADS = 32
DN_CONV = 5
DN_CHUNK = 64

V7X_VMEM_BYTES = 64 * 1024 * 1024
VMEM_LIMIT = V7X_VMEM_BYTES - 8 * 1024 * 1024
SUBLANES = 8

NT_DIMS = (((1,), (1,)), ((), ()))
TN_DIMS = (((0,), (0,)), ((), ()))


def _params(*sem):
    return pltpu.CompilerParams(dimension_semantics=sem, vmem_limit_bytes=VMEM_LIMIT)


def _silu(x):
    return x * jax.nn.sigmoid(x)


def _split3(x):
    hi = x.astype(BF16)
    r1 = x - hi.astype(F32)
    mid = r1.astype(BF16)
    lo = (r1 - mid.astype(F32)).astype(BF16)
    return hi, mid, lo


def _div_pow2(x, n):
    assert n & (n - 1) == 0
    return jnp.right_shift(x, n.bit_length() - 1)


def _mm(a, b):
    return jnp.dot(a.astype(BF16), b.astype(BF16), preferred_element_type=F32)


def _ada_kernel(cond_ref, w_ref, b_ref, o_ref):
    a_hi, a_mid, a_lo = _split3(_silu(cond_ref[...]))
    w_hi, w_mid, w_lo = _split3(w_ref[0])
    dot = functools.partial(jnp.dot, preferred_element_type=F32)
    acc = dot(a_hi, w_hi) + (dot(a_hi, w_mid) + dot(a_mid, w_hi)) + (
        dot(a_hi, w_lo) + dot(a_mid, w_mid) + dot(a_lo, w_hi))
    o_ref[0] = acc + b_ref[0]


def ada_mod(cond, ada_w, ada_b, *, tn=768):
    depth, d, n = ada_w.shape
    r = cond.shape[0]
    return pl.pallas_call(
        _ada_kernel,
        grid=(depth, n // tn),
        in_specs=[pl.BlockSpec((r, d), lambda i, j: (0, 0)),
                  pl.BlockSpec((1, d, tn), lambda i, j: (i, 0, j)),
                  pl.BlockSpec((1, 1, tn), lambda i, j: (i, 0, j))],
        out_specs=pl.BlockSpec((1, r, tn), lambda i, j: (i, 0, j)),
        out_shape=jax.ShapeDtypeStruct((depth, r, n), F32),
        compiler_params=_params("parallel", "parallel"),
        name="ada_mod",
    )(cond, ada_w, ada_b.reshape(depth, 1, n))


NORM_ROWS = 128


def _norm_mm_kernel(x_ref, nw_ref, sh_ref, sc_ref, w_ref, *rest, ranges):
    out_refs, h_ref = rest[:-1], rest[-1]
    j = pl.program_id(1)

    @pl.when(j == 0)
    def _():
        def rows(i, carry):
            r = pl.ds(pl.multiple_of(i * NORM_ROWS, NORM_ROWS), NORM_ROWS)
            x = x_ref[r, :]
            y = x * lax.rsqrt(jnp.mean(x * x, axis=-1, keepdims=True) + EPS) * nw_ref[...]
            h_ref[r, :] = (y * (1.0 + sc_ref[0]) + sh_ref[0]).astype(BF16)
            return carry

        lax.fori_loop(0, x_ref.shape[0] // NORM_ROWS, rows, 0)

    acc = jnp.dot(h_ref[...], w_ref[...], preferred_element_type=F32)
    for (start, ntiles), o_ref in zip(ranges, out_refs):
        @pl.when((j >= start) & (j < start + ntiles))
        def _(o_ref=o_ref):
            o_ref[...] = acc.astype(o_ref.dtype)


def norm_matmul(x, norm_w, shift, scale, w, outs, *, seq_len, tm=512, tn=1024):
    m, d = x.shape
    n = w.shape[1]
    nb = shift.shape[0]
    tm, tn = min(tm, m), min(tn, n)
    ranges, start = [], 0
    for width, _ in outs:
        ranges.append((start, width // tn))
        start += width // tn
    assert start * tn == n and m % tm == 0 and (nb == 1 or seq_len % tm == 0)

    def mod_map(i, j):
        return ((i * tm) // seq_len if nb > 1 else 0, 0, 0)

    def out_map(i, j, s, t):
        return (i, jnp.clip(j - s, 0, t - 1))

    return pl.pallas_call(
        functools.partial(_norm_mm_kernel, ranges=tuple(ranges)),
        grid=(m // tm, n // tn),
        in_specs=[pl.BlockSpec((tm, d), lambda i, j: (i, 0)),
                  pl.BlockSpec((1, d), lambda i, j: (0, 0)),
                  pl.BlockSpec((1, 1, d), mod_map),
                  pl.BlockSpec((1, 1, d), mod_map),
                  pl.BlockSpec((d, tn), lambda i, j: (0, j))],
        out_specs=[pl.BlockSpec((tm, tn), functools.partial(out_map, s=s, t=t)) for s, t in ranges],
        out_shape=[jax.ShapeDtypeStruct((m, width), dt) for width, dt in outs],
        scratch_shapes=[pltpu.VMEM((tm, d), BF16)],
        compiler_params=_params("parallel", "arbitrary"),
        name="norm_matmul",
    )(x, norm_w.reshape(1, d), shift, scale, w)


def _out_kernel(g_ref, w_ref, x_ref, gate_ref, fw_ref, o_ref, *, final):
    acc = jnp.dot(g_ref[...], w_ref[...], preferred_element_type=F32)
    xn = x_ref[...] + gate_ref[0] * acc
    if final:
        xn = xn * lax.rsqrt(jnp.mean(xn * xn, axis=-1, keepdims=True) + EPS) * fw_ref[...]
    o_ref[...] = xn


def out_residual(g, w, x, gate, final_w, *, seq_len, final, tm):
    m, k = g.shape
    d = w.shape[1]
    nb = gate.shape[0]
    assert m % tm == 0 and (nb == 1 or seq_len % tm == 0)

    def mod_map(i):
        return ((i * tm) // seq_len if nb > 1 else 0, 0, 0)

    return pl.pallas_call(
        functools.partial(_out_kernel, final=final),
        grid=(m // tm,),
        in_specs=[pl.BlockSpec((tm, k), lambda i: (i, 0)),
                  pl.BlockSpec((k, d), lambda i: (0, 0)),
                  pl.BlockSpec((tm, d), lambda i: (i, 0)),
                  pl.BlockSpec((1, 1, d), mod_map),
                  pl.BlockSpec((1, d), lambda i: (0, 0))],
        out_specs=pl.BlockSpec((tm, d), lambda i: (i, 0)),
        out_shape=jax.ShapeDtypeStruct((m, d), F32),
        compiler_params=_params("parallel"),
        name="out_residual",
    )(g, w, x, gate, final_w.reshape(1, d))


def _ctx_attn_kernel(q_ref, k_ref, v_ref, z_ref, o_ref, *, heads):
    scale = HEAD_DIM ** -0.5
    for h in range(heads):
        sl = slice(h * HEAD_DIM, (h + 1) * HEAD_DIM)
        q = q_ref[:, sl]
        k = k_ref[:, sl].astype(BF16)
        v = v_ref[:, sl].astype(BF16)
        s = lax.dot_general(q, k, NT_DIMS, preferred_element_type=F32) * scale
        p = jnp.exp(s - jnp.max(s, axis=-1, keepdims=True))
        o = jnp.dot(p.astype(BF16), v, preferred_element_type=F32) / jnp.sum(p, axis=-1, keepdims=True)
        o_ref[:, sl] = (o * _silu(z_ref[:, sl].astype(F32))).astype(BF16)


def context_attention(q, k, v, z, *, seq_len):
    m, width = q.shape
    spec = pl.BlockSpec((seq_len, width), lambda b: (b, 0))
    return pl.pallas_call(
        functools.partial(_ctx_attn_kernel, heads=width // HEAD_DIM),
        grid=(m // seq_len,),
        in_specs=[spec, spec, spec, spec],
        out_specs=spec,
        out_shape=jax.ShapeDtypeStruct((m, width), BF16),
        compiler_params=_params("parallel"),
        name="context_attention",
    )(q, k, v, z)


ATTN_Q_ROWS = 4
ATTN_GROUP = 2


def _attn_plan(rows):
    wr = min(WIN_R, rows)
    qb = min(ATTN_Q_ROWS, rows)
    kw = min(qb + wr - 1 + (qb + wr - 1) % 2, rows)
    assert rows % qb == 0 and kw % 2 == 0

    def window_start(r):
        return min(max(r - wr // 2, 0), rows - wr)

    blocks, patterns = [], []
    for r0 in range(0, rows, qb):
        k0 = min(max(r0 - wr // 2, 0), rows - kw)
        pat = tuple(tuple((k0 + i) - (r0 + j) + WIN_R - 1
                          if window_start(r0 + j) <= k0 + i < window_start(r0 + j) + wr else None
                          for i in range(kw)) for j in range(qb))
        assert all(sum(d is not None for d in row) == wr for row in pat)
        if pat not in patterns:
            patterns.append(pat)
        blocks.append((r0, k0, patterns.index(pat)))
    return qb, kw, blocks, patterns


def _bias_kernel(rpb_ref, o_ref, *, rows):
    h = pl.program_id(0)
    qb, kw, _, patterns = _attn_plan(rows)
    n_dr, n_dc = 2 * WIN_R - 1, 2 * WIN_C - 1
    lane = lax.broadcasted_iota(jnp.int32, (GRID_W, 2 * GRID_W), 1)
    qi = lax.broadcasted_iota(jnp.int32, (GRID_W, 2 * GRID_W), 0)
    kcol = lane & (GRID_W - 1)
    second = lane >= GRID_W
    dc = kcol - qi + (WIN_C - 1)
    col_start = jnp.clip(qi - WIN_C // 2, 0, GRID_W - WIN_C)
    col_in = (kcol >= col_start) & (kcol < col_start + WIN_C)
    cache = {}

    def tile(d_lo, d_hi):
        if (d_lo, d_hi) not in cache:
            acc = jnp.zeros((GRID_W, 2 * GRID_W), F32)
            ok = col_in
            if d_lo is None:
                ok = ok & second
            if d_hi is None:
                ok = ok & jnp.logical_not(second)
            if d_lo is not None or d_hi is not None:
                for c in range(n_dc):
                    lo = rpb_ref[(h * n_dr + d_lo) * n_dc + c] if d_lo is not None else 0.0
                    hi = rpb_ref[(h * n_dr + d_hi) * n_dc + c] if d_hi is not None else 0.0
                    acc = jnp.where(dc == c, jnp.where(second, hi, lo), acc)
            cache[(d_lo, d_hi)] = jnp.where(ok, acc, -jnp.inf)
        return cache[(d_lo, d_hi)]

    n_ctx = o_ref.shape[3] - kw * GRID_W
    for t, pat in enumerate(patterns):
        for j in range(qb):
            for i in range(0, kw, 2):
                o_ref[0, t, j * GRID_W:(j + 1) * GRID_W, i * GRID_W:(i + 2) * GRID_W] = tile(pat[j][i], pat[j][i + 1])
        o_ref[0, t, :, kw * GRID_W:] = jnp.zeros((qb * GRID_W, n_ctx), F32)


def window_bias(rpb, *, rows, ctx_len):
    heads = rpb.shape[0]
    qb, kw, _, patterns = _attn_plan(rows)
    shape = (len(patterns), qb * GRID_W, kw * GRID_W + ctx_len)
    return pl.pallas_call(
        functools.partial(_bias_kernel, rows=rows),
        grid=(heads,),
        in_specs=[pl.BlockSpec(memory_space=pltpu.SMEM)],
        out_specs=pl.BlockSpec((1,) + shape, lambda h: (h, 0, 0, 0)),
        out_shape=jax.ShapeDtypeStruct((heads,) + shape, F32),
        compiler_params=_params("parallel"),
        name="window_bias",
    )(rpb.reshape(-1))


def _lat_attn_kernel(q_ref, k_ref, v_ref, z_ref, ck_ref, cv_ref, tbl_ref, o_ref, *, rows):
    scale = HEAD_DIM ** -0.5
    qb, kw, blocks, _ = _attn_plan(rows)
    ck = ck_ref[0].astype(BF16)
    cv = cv_ref[0].astype(BF16)
    for g in range(0, len(blocks), ATTN_GROUP):
        grp = blocks[g:g + ATTN_GROUP]
        qrows = [slice(r0 * GRID_W, (r0 + qb) * GRID_W) for r0, _, _ in grp]
        krows = [slice(k0 * GRID_W, (k0 + kw) * GRID_W) for _, k0, _ in grp]
        s = [lax.dot_general(q_ref[qr, :], jnp.concatenate([k_ref[kr, :], ck], axis=0), NT_DIMS,
                             preferred_element_type=F32) * scale + tbl_ref[0, t]
             for qr, kr, (_, _, t) in zip(qrows, krows, grp)]
        p = [jnp.exp(x - jnp.max(x, axis=-1, keepdims=True)) for x in s]
        o = [jnp.dot(x.astype(BF16), jnp.concatenate([v_ref[kr, :], cv], axis=0), preferred_element_type=F32)
             / jnp.sum(x, axis=-1, keepdims=True) for x, kr in zip(p, krows)]
        for x, qr in zip(o, qrows):
            o_ref[qr, :] = (x * _silu(z_ref[qr, :].astype(F32))).astype(BF16)


def neighbourhood_attention(q, k, v, z, ck, cv, bias, *, seq_len):
    m, width = q.shape
    heads = width // HEAD_DIM
    rows = seq_len // GRID_W
    ctx_len = ck.shape[1]
    tok = pl.BlockSpec((seq_len, HEAD_DIM), lambda h, b: (b, h))
    ctx = pl.BlockSpec((1, ctx_len, HEAD_DIM), lambda h, b: (b, 0, h))
    return pl.pallas_call(
        functools.partial(_lat_attn_kernel, rows=rows),
        grid=(heads, m // seq_len),
        in_specs=[tok, tok, tok, tok, ctx, ctx,
                  pl.BlockSpec((1,) + bias.shape[1:], lambda h, b: (h, 0, 0, 0))],
        out_specs=tok,
        out_shape=jax.ShapeDtypeStruct((m, width), BF16),
        compiler_params=_params("parallel", "parallel"),
        name="neighbourhood_attention",
    )(q, k, v, z, ck, cv, bias)


CONV_PAD = 16


def _conv_kernel(x_ref, w_ref, o_ref, pad_ref, *, l2_scale, rows_per_step):
    t = x_ref.shape[1]
    tc = x_ref.shape[2]
    rps = rows_per_step
    half = DN_CONV // 2
    win = rps + 2 * CONV_PAD
    pad_ref[0:CONV_PAD, :] = jnp.zeros((CONV_PAD, tc), BF16)
    pad_ref[CONV_PAD + t:2 * CONV_PAD + t, :] = jnp.zeros((CONV_PAD, tc), BF16)
    pad_ref[CONV_PAD:CONV_PAD + t, :] = x_ref[0]

    taps = [j for j in range(DN_CONV) if j != half]
    r = lax.broadcasted_iota(jnp.int32, (len(taps) * rps, win), 0)
    c = lax.broadcasted_iota(jnp.int32, (len(taps) * rps, win), 1)
    src = (r & (rps - 1)) + CONV_PAD - half
    for m, j in enumerate(taps):
        src = src + jnp.where(_div_pow2(r, rps) == m, j, 0)
    shift = jnp.where(c == src, 1.0, 0.0).astype(BF16)

    for r0 in range(0, t, rps):
        xw = pad_ref[r0:r0 + win, :]
        moved = jnp.dot(shift, xw, preferred_element_type=F32)
        acc = xw[CONV_PAD:CONV_PAD + rps].astype(F32) * w_ref[half:half + 1, :]
        for m, j in enumerate(taps):
            acc = acc + moved[m * rps:(m + 1) * rps] * w_ref[j:j + 1, :]
        y = _silu(acc)
        if l2_scale is not None:
            parts = []
            for g in range(tc // HEAD_DIM):
                yg = y[:, g * HEAD_DIM:(g + 1) * HEAD_DIM]
                parts.append(yg * (lax.rsqrt(jnp.sum(yg * yg, axis=-1, keepdims=True) + EPS) * l2_scale))
            y = jnp.concatenate(parts, axis=1)
        o_ref[0, r0:r0 + rows_per_step, :] = y.astype(BF16)


def conv_silu(x, conv_w, *, seq_len, col0, l2_scale, tc=512, rows_per_step=128):
    m, c = x.shape
    b = m // seq_len
    rows_per_step = min(rows_per_step, seq_len)
    out = pl.pallas_call(
        functools.partial(_conv_kernel, l2_scale=l2_scale, rows_per_step=rows_per_step),
        grid=(b, c // tc),
        in_specs=[pl.BlockSpec((1, seq_len, tc), lambda i, j: (i, 0, j)),
                  pl.BlockSpec((DN_CONV, tc), lambda i, j: (0, col0 // tc + j))],
        out_specs=pl.BlockSpec((1, seq_len, tc), lambda i, j: (i, 0, j)),
        out_shape=jax.ShapeDtypeStruct((b, seq_len, c), BF16),
        scratch_shapes=[pltpu.VMEM((seq_len + 2 * CONV_PAD, tc), BF16)],
        compiler_params=_params("parallel", "parallel"),
        name="conv_silu",
    )(x.reshape(b, seq_len, c), conv_w)
    return out.reshape(m, c)


def _gate_kernel(ba_ref, alog_ref, dtb_ref, o_ref):
    x = ba_ref[...]
    tm = x.shape[0]
    lane = lax.broadcasted_iota(jnp.int32, x.shape, 1)
    beta = jax.nn.sigmoid(x)
    xs = x + dtb_ref[...]
    softplus = jnp.maximum(xs, 0.0) + jnp.log1p(jnp.exp(-jnp.abs(xs)))
    g = jnp.where(lane >= 2 * DN_V_HEADS, -jnp.exp(alog_ref[...]) * softplus, 0.0)
    r = lax.broadcasted_iota(jnp.int32, (tm, tm), 0)
    c = lax.broadcasted_iota(jnp.int32, (tm, tm), 1)
    same = _div_pow2(r, DN_CHUNK) == _div_pow2(c, DN_CHUNK)
    lower = jnp.where(same & (c <= r), 1.0, 0.0).astype(BF16)
    upper = jnp.where(same & (c >= r), 1.0, 0.0).astype(BF16)
    dot = functools.partial(jnp.dot, preferred_element_type=F32)
    hi, mid, lo = _split3(g)
    cf = dot(lower, hi) + dot(lower, mid) + dot(lower, lo)
    cb = dot(upper, hi) + dot(upper, mid) + dot(upper, lo)
    gc = jnp.where(lane < 3 * DN_V_HEADS, cf, cb)
    o_ref[...] = jnp.where(lane < 2 * DN_V_HEADS, beta, gc)


def gate_sums(ba, a_log, dt_bias, *, tm=256):
    m, w = ba.shape
    zeros = jnp.zeros((2 * DN_V_HEADS,), F32)
    alog = jnp.concatenate([zeros, a_log.reshape(-1)]).reshape(1, w)
    dtb = jnp.concatenate([zeros, dt_bias.reshape(-1)]).reshape(1, w)
    return pl.pallas_call(
        _gate_kernel,
        grid=(m // tm,),
        in_specs=[pl.BlockSpec((tm, w), lambda i: (i, 0)),
                  pl.BlockSpec((1, w), lambda i: (0, 0)),
                  pl.BlockSpec((1, w), lambda i: (0, 0))],
        out_specs=pl.BlockSpec((tm, w), lambda i: (i, 0)),
        out_shape=jax.ShapeDtypeStruct((m, w), F32),
        compiler_params=_params("parallel"),
        name="gate_sums",
    )(ba, alog, dtb)


N_PROB = 4
PREPARE_GROUP = 8


def _block_diag(x, mask):
    xb = x.astype(BF16)
    return jnp.where(mask, jnp.concatenate([xb] * N_PROB, axis=0), jnp.zeros((), BF16))


def _unit_tri_inverse(a_list, row, col, eye, bd_mask):
    cs = a_list[0].shape[0]

    def same_block(b):
        return _div_pow2(row, b) == _div_pow2(col, b)

    def mm(xs, ys_bd):
        return [jnp.dot(x.astype(BF16), y, preferred_element_type=F32) for x, y in zip(xs, ys_bd)]

    def bd(xs):
        return [_block_diag(x, bd_mask) for x in xs]

    base = 8
    d = [jnp.where(same_block(base), a, 0.0) for a in a_list]
    d2 = mm(d, bd(d))
    d2_bd = bd(d2)
    d4_bd = bd(mm(d2, d2_bd))
    t = [eye - x for x in d]
    t = [x + y for x, y in zip(t, mm(t, d2_bd))]
    t = [x + y for x, y in zip(t, mm(t, d4_bd))]
    b = base
    while b < cs:
        join = same_block(2 * b) & jnp.logical_not(same_block(b))
        e_bd = bd([jnp.where(join, a, 0.0) for a in a_list])
        t = [x - y for x, y in zip(t, mm(mm(t, e_bd), bd(t)))]
        b *= 2
    return t


def _delta_kernel(*refs, n_chunks, hb, has_init, emit_state):
    q_ref, k_ref, v_ref, z_ref, pc_ref, pr_ref, nw_ref = refs[:7]
    refs = refs[7:]
    s0_ref = None
    if has_init:
        s0_ref, refs = refs[0], refs[1:]
    o_ref, refs = refs[0], refs[1:]
    sn_ref = None
    if emit_state:
        sn_ref, refs = refs[0], refs[1:]
    acc_ref, s_ref, n_ref, pm_ref = refs
    cs, hd = DN_CHUNK, HEAD_DIM
    wide = N_PROB * cs

    row = lax.broadcasted_iota(jnp.int32, (cs, wide), 0)
    lane_d = lax.broadcasted_iota(jnp.int32, (hd, wide), 1)
    lane = lax.broadcasted_iota(jnp.int32, (cs, wide), 1)
    col = lane & (cs - 1)
    fwd = lane < 2 * cs
    strict = (fwd & (col < row)) | (jnp.logical_not(fwd) & (col > row))
    incl = strict | (col == row)
    eye = jnp.where(col == row, 1.0, 0.0)
    first_half = (lax.broadcasted_iota(jnp.int32, (cs, 2 * cs), 1) < cs)
    bd_mask = (_div_pow2(lax.broadcasted_iota(jnp.int32, (wide, wide), 0), cs)
               == _div_pow2(lax.broadcasted_iota(jnp.int32, (wide, wide), 1), cs))

    def side_by_side(c0, c1, c2, c3):
        half = first_half[:c0.shape[0]]
        return jnp.concatenate([jnp.where(half, c0, c1), jnp.where(half, c2, c3)], axis=1)

    group = min(max(PREPARE_GROUP // hb, 1), n_chunks)
    units = [(cj, hh) for cj in range(group) for hh in range(hb)]
    each = range(len(units))
    head = [hh for _, hh in units]

    def prepare(gi, carry):
        cidx = [gi * group + cj for cj, _ in units]
        rows = [pl.ds(pl.multiple_of(c * cs, cs), cs) for c in cidx]
        qc = [q_ref[r, hh * hd:(hh + 1) * hd] for r, hh in zip(rows, head)]
        kc = [k_ref[r, hh * hd:(hh + 1) * hd] for r, hh in zip(rows, head)]
        kf = [x.astype(F32) for x in kc]
        pc = [pc_ref[0, hh, r, :] for r, hh in zip(rows, head)]
        gcr = [pr_ref[0, hh, c] for c, hh in zip(cidx, head)]
        beta = [[x[:, p:p + 1] for p in range(N_PROB)] for x in pc]
        gcc = [[x[:, N_PROB + p:N_PROB + p + 1] for p in range(N_PROB)] for x in pc]
        prod = [lax.dot_general(jnp.concatenate([qc[j], kc[j]], axis=0), jnp.concatenate([kc[j]] * N_PROB, axis=0),
                                NT_DIMS, preferred_element_type=F32) for j in each]
        decay = [jnp.exp(jnp.where(incl, side_by_side(*gcc[j]) - gcr[j], -jnp.inf)) for j in each]
        a = [jnp.where(strict, side_by_side(*beta[j]) * prod[j][cs:] * decay[j], 0.0) for j in each]
        t = _unit_tri_inverse(a, row, col, eye, bd_mask)
        rhs = []
        for j in each:
            parts = []
            for p in range(N_PROB):
                vcol = (2 * head[j] + p % 2) * hd
                vf = v_ref[rows[j], vcol:vcol + hd].astype(F32)
                parts.append(jnp.concatenate([vf * beta[j][p], kf[j] * (beta[j][p] * jnp.exp(gcc[j][p]))], axis=1))
            rhs.append(jnp.concatenate(parts, axis=0).astype(BF16))
        sol = [jnp.dot(_block_diag(t[j], bd_mask), rhs[j], preferred_element_type=F32).astype(BF16)
               for j in each]
        osol = [jnp.dot(_block_diag(prod[j][:cs] * decay[j], bd_mask), sol[j], preferred_element_type=F32)
                for j in each]
        ksol = []
        for j in each:
            g_last = [gcc[j][p][(cs - 1 if p < 2 else 0):(cs if p < 2 else 1), :] for p in range(N_PROB)]
            kdt = jnp.concatenate([kf[j]] * N_PROB, axis=0).T * jnp.exp(side_by_side(*g_last) - gcr[j])
            lhs = jnp.concatenate([jnp.where((lane_d >= p * cs) & (lane_d < (p + 1) * cs), kdt, 0.0)
                                   for p in range(N_PROB)], axis=0).astype(BF16)
            ksol.append(jnp.dot(lhs, sol[j], preferred_element_type=F32))
        for j in each:
            qf = qc[j].astype(F32)
            for p in range(N_PROB):
                kp = ksol[j][p * hd:(p + 1) * hd]
                op = osol[j][p * cs:(p + 1) * cs]
                n_ref[cidx[j], head[j] * N_PROB + p] = kp[:, :hd]
                pm_ref[cidx[j], head[j] * N_PROB + p, 0:hd] = (-kp[:, hd:]).astype(BF16)
                pm_ref[cidx[j], head[j] * N_PROB + p, hd:hd + cs] = (
                    qf * jnp.exp(gcc[j][p]) - op[:, hd:]).astype(BF16)
            for e in range(2):
                ocol = (2 * head[j] + e) * hd
                acc_ref[rows[j], ocol:ocol + hd] = (osol[j][e * cs:(e + 1) * cs, :hd]
                                                    + osol[j][(2 + e) * cs:(3 + e) * cs, :hd])
        return carry

    lax.fori_loop(0, n_chunks // group, prepare, 0)

    problems = [(hh, p) for hh in range(hb) for p in range(N_PROB)]
    for hh, p in problems:
        s_ref[hh * N_PROB + p] = s0_ref[0, p // 2, 2 * hh + p % 2] if has_init else jnp.zeros((hd, hd), F32)

    def advance(ci, carry):
        for hh, p in problems:
            ocol = (2 * hh + p % 2) * hd
            c = ci if p < 2 else n_chunks - 1 - ci
            last = cs - 1 if p < 2 else 0
            rows = pl.ds(pl.multiple_of(c * cs, cs), cs)
            g_last = pc_ref[0, hh, pl.ds(c * cs + last, 1), N_PROB + p:N_PROB + p + 1]
            s = s_ref[hh * N_PROB + p]
            ws = jnp.dot(pm_ref[c, hh * N_PROB + p], s.astype(BF16),
                         preferred_element_type=F32)
            s_ref[hh * N_PROB + p] = s * jnp.exp(g_last) + n_ref[c, hh * N_PROB + p] + ws[:hd]
            acc_ref[rows, ocol:ocol + hd] += ws[hd:]
        return carry

    lax.fori_loop(0, n_chunks, advance, 0)

    if emit_state:
        for hh, p in problems:
            sn_ref[0, 0, p // 2, 2 * hh + p % 2] = s_ref[hh * N_PROB + p]
    for e in range(2 * hb):
        sl = slice(e * HEAD_DIM, (e + 1) * HEAD_DIM)
        o = acc_ref[:, sl]
        y = o * lax.rsqrt(jnp.mean(o * o, axis=-1, keepdims=True) + EPS) * nw_ref[...]
        o_ref[:, sl] = (y * _silu(z_ref[:, sl].astype(F32))).astype(BF16)


def gated_delta(q, k, v, z, p, norm_w, s0, *, seq_len, emit_state):
    m = q.shape[0]
    b = m // seq_len
    n_chunks = seq_len // DN_CHUNK
    rep = DN_V_HEADS // DN_K_HEADS
    assert 2 * rep == N_PROB
    p5 = p.reshape(b, seq_len, 2, 2, DN_K_HEADS, rep)
    pc = p5.transpose(0, 4, 1, 2, 3, 5).reshape(b, DN_K_HEADS, seq_len, 2 * N_PROB)
    pr = p5[:, :, 1].reshape(b, n_chunks, DN_CHUNK, 2, DN_K_HEADS, rep).transpose(0, 4, 1, 3, 5, 2).reshape(
        b, DN_K_HEADS, n_chunks, 1, N_PROB * DN_CHUNK)
    d = HEAD_DIM
    per_head = (n_chunks * N_PROB * (d * d * 4 + (d + DN_CHUNK) * d * 2) + seq_len * rep * d * 4
                + 2 * seq_len * (2 * d * 2 + 3 * rep * d * 2 + HEAD_DIM * 4))
    hb = max(h for h in (1, 2) if h == 1 or h * per_head <= VMEM_LIMIT // 2)
    in_specs = [pl.BlockSpec((seq_len, hb * d), lambda i, h: (i, h)),
                pl.BlockSpec((seq_len, hb * d), lambda i, h: (i, h)),
                pl.BlockSpec((seq_len, hb * rep * d), lambda i, h: (i, h)),
                pl.BlockSpec((seq_len, hb * rep * d), lambda i, h: (i, h)),
                pl.BlockSpec((1, hb, seq_len, 2 * N_PROB), lambda i, h: (i, h, 0, 0)),
                pl.BlockSpec((1, hb, n_chunks, 1, N_PROB * DN_CHUNK), lambda i, h: (i, h, 0, 0, 0)),
                pl.BlockSpec((1, d), lambda i, h: (0, 0))]
    args = [q, k, v, z, pc, pr, norm_w.reshape(1, d)]
    if s0 is not None:
        in_specs.append(pl.BlockSpec((1, 2, hb * rep, d, d), lambda i, h: (i, 0, h, 0, 0)))
        args.append(s0)
    out_specs = [pl.BlockSpec((seq_len, hb * rep * d), lambda i, h: (i, h))]
    out_shape = [jax.ShapeDtypeStruct((m, DN_V_HEADS * d), BF16)]
    if emit_state:
        out_specs.append(pl.BlockSpec((1, 1, 2, hb * rep, d, d), lambda i, h: (i, 0, 0, h, 0, 0)))
        out_shape.append(jax.ShapeDtypeStruct((b, 1, 2, DN_V_HEADS, d, d), F32))
    res = pl.pallas_call(
        functools.partial(_delta_kernel, n_chunks=n_chunks, hb=hb, has_init=s0 is not None,
                          emit_state=emit_state),
        grid=(b, DN_K_HEADS // hb),
        in_specs=in_specs,
        out_specs=out_specs,
        out_shape=out_shape,
        scratch_shapes=[pltpu.VMEM((seq_len, hb * rep * d), F32),
                        pltpu.VMEM((hb * N_PROB, d, d), F32),
                        pltpu.VMEM((n_chunks, hb * N_PROB, d, d), F32),
                        pltpu.VMEM((n_chunks, hb * N_PROB, d + DN_CHUNK, d), BF16)],
        compiler_params=_params("parallel", "parallel"),
        name="gated_delta",
    )(*args)
    return res if emit_state else (res[0], None)


def _trunk(x, mod, rows, cache_k, cache_v, state, weights, *, emit):
    (norm_w, na_w_in, na_w_out, bias, dn_w_in, dn_w_ba, dn_conv_w, dn_a_log, dn_dt_bias,
     dn_norm_w, dn_w_out, final_norm_w) = weights
    b, t, d = x.shape
    x2 = x.reshape(b * t, d)
    kw = dict(seq_len=t)

    def mods(layer):
        mm = mod[layer, rows][:, None, :]
        return mm[..., :d], mm[..., d:2 * d], mm[..., 2 * d:]

    shift, scale, gate = mods(0)
    local = cache_k is not None
    kv_dt = BF16 if local else F32
    q, k, v, z = norm_matmul(x2, norm_w[0], shift, scale, na_w_in,
                             ((d, BF16), (d, kv_dt), (d, kv_dt), (d, BF16)), tm=1024 if local else 512, **kw)
    if local:
        g = neighbourhood_attention(q, k, v, z, cache_k, cache_v, bias, **kw)
    else:
        g = context_attention(q, k, v, z, **kw)
    x2 = out_residual(g, na_w_out, x2, gate, final_norm_w, final=False, tm=512, **kw)

    shift, scale, gate = mods(1)
    kw_dim, vw_dim = DN_K_HEADS * HEAD_DIM, DN_V_HEADS * HEAD_DIM
    qp, kp, vp, zz = norm_matmul(x2, norm_w[1], shift, scale, dn_w_in,
                                 ((kw_dim, BF16), (kw_dim, BF16), (vw_dim, BF16), (vw_dim, BF16)), tm=1024, **kw)
    (ba,) = norm_matmul(x2, norm_w[1], shift, scale, dn_w_ba, ((4 * DN_V_HEADS, F32),), **kw)
    qn = conv_silu(qp, dn_conv_w, col0=0, l2_scale=HEAD_DIM ** -0.5, **kw)
    kn = conv_silu(kp, dn_conv_w, col0=kw_dim, l2_scale=1.0, **kw)
    vn = conv_silu(vp, dn_conv_w, col0=2 * kw_dim, l2_scale=None, **kw)
    p = gate_sums(ba, dn_a_log, dn_dt_bias)
    og, s_new = gated_delta(qn, kn, vn, zz, p, dn_norm_w, state, emit_state=emit, **kw)
    y = out_residual(og, dn_w_out, x2, gate, final_norm_w, final=True, tm=256, **kw)
    return y.reshape(b, t, d), k, v, s_new


def kernel(x_prompt, x_sample, cache_na_k, cache_na_v, state_dn, c, c_ctx, norm_w, ada_w, ada_b, na_w_in, na_w_out, na_rpb, dn_w_in, dn_conv_w, dn_a_log, dn_dt_bias, dn_norm_w, dn_w_out, final_norm_w):
    bp, tp, d = x_prompt.shape
    bs, ts, _ = x_sample.shape
    ctx_len = cache_na_k.shape[2]
    assert norm_w.shape[0] == 2 and na_w_in.shape[0] == 1 and dn_w_in.shape[0] == 1

    n_cond = -(-(bs + 1) // SUBLANES) * SUBLANES
    cond = jnp.concatenate([c, c_ctx[None], jnp.zeros((n_cond - bs - 1, d), F32)])
    mod = ada_mod(cond, ada_w, ada_b)

    conv_width = 2 * DN_K_HEADS * HEAD_DIM + DN_V_HEADS * HEAD_DIM
    n_main = conv_width + DN_V_HEADS * HEAD_DIM
    weights = (norm_w, na_w_in[0].astype(BF16), na_w_out[0].astype(BF16),
               window_bias(na_rpb[0], rows=ts // GRID_W, ctx_len=ctx_len),
               dn_w_in[0, :, :n_main].astype(BF16), dn_w_in[0, :, n_main:].astype(BF16),
               dn_conv_w[0], dn_a_log[0], dn_dt_bias[0], dn_norm_w[0], dn_w_out[0].astype(BF16),
               final_norm_w)

    y_prompt, k_new, v_new, s_new = _trunk(
        x_prompt, mod, slice(bs, bs + 1), None, None, None, weights, emit=True)
    y_sample, _, _, _ = _trunk(
        x_sample, mod, slice(0, bs), cache_na_k[:, 0].reshape(bs, ctx_len, d),
        cache_na_v[:, 0].reshape(bs, ctx_len, d), state_dn[:, 0], weights, emit=False)

    kv_shape = (bp, 1, tp, NA_HEADS, HEAD_DIM)
    return (y_prompt, y_sample, k_new.reshape(kv_shape), v_new.reshape(kv_shape), s_new)
```

```python
import functools

import jax
import jax.numpy as jnp
from jax import lax
from jax.experimental import pallas as pl
from jax.experimental.pallas import tpu as pltpu

F32 = jnp.float32
BF16 = jnp.bfloat16

EPS = 1e-6
GRID_W = 64
NA_HEADS = 16
HEAD_DIM = 128
WIN_R = 8
WIN_C = 16
DN_K_HEADS = 16
DN_V_HEADS = 32
DN_CONV = 5
DN_CHUNK = 64

V7X_VMEM_BYTES = 64 * 1024 * 1024
VMEM_LIMIT = V7X_VMEM_BYTES - 8 * 1024 * 1024
SUBLANES = 8

NT_DIMS = (((1,), (1,)), ((), ()))
TN_DIMS = (((0,), (0,)), ((), ()))


def _params(*sem):
    return pltpu.CompilerParams(dimension_semantics=sem, vmem_limit_bytes=VMEM_LIMIT)


def _silu(x):
    return x * jax.nn.sigmoid(x)


def _split3(x):
    hi = x.astype(BF16)
    r1 = x - hi.astype(F32)
    mid = r1.astype(BF16)
    lo = (r1 - mid.astype(F32)).astype(BF16)
    return hi, mid, lo


def _div_pow2(x, n):
    assert n & (n - 1) == 0
    return jnp.right_shift(x, n.bit_length() - 1)


def _mm(a, b):
    return jnp.dot(a.astype(BF16), b.astype(BF16), preferred_element_type=F32)


def _ada_kernel(cond_ref, w_ref, b_ref, o_ref):
    a_hi, a_mid, a_lo = _split3(_silu(cond_ref[...]))
    w_hi, w_mid, w_lo = _split3(w_ref[0])
    dot = functools.partial(jnp.dot, preferred_element_type=F32)
    acc = dot(a_hi, w_hi) + (dot(a_hi, w_mid) + dot(a_mid, w_hi)) + (
        dot(a_hi, w_lo) + dot(a_mid, w_mid) + dot(a_lo, w_hi))
    o_ref[0] = acc + b_ref[0]


def ada_mod(cond, ada_w, ada_b, *, tn=768):
    depth, d, n = ada_w.shape
    r = cond.shape[0]
    return pl.pallas_call(
        _ada_kernel,
        grid=(depth, n // tn),
        in_specs=[pl.BlockSpec((r, d), lambda i, j: (0, 0)),
                  pl.BlockSpec((1, d, tn), lambda i, j: (i, 0, j)),
                  pl.BlockSpec((1, 1, tn), lambda i, j: (i, 0, j))],
        out_specs=pl.BlockSpec((1, r, tn), lambda i, j: (i, 0, j)),
        out_shape=jax.ShapeDtypeStruct((depth, r, n), F32),
        compiler_params=_params("parallel", "parallel"),
        name="ada_mod",
    )(cond, ada_w, ada_b.reshape(depth, 1, n))


NORM_ROWS = 128


def _norm_mm_kernel(x_ref, nw_ref, sh_ref, sc_ref, w_ref, *rest, ranges):
    out_refs, h_ref = rest[:-1], rest[-1]
    j = pl.program_id(1)

    @pl.when(j == 0)
    def _():
        def rows(i, carry):
            r = pl.ds(pl.multiple_of(i * NORM_ROWS, NORM_ROWS), NORM_ROWS)
            x = x_ref[r, :]
            y = x * lax.rsqrt(jnp.mean(x * x, axis=-1, keepdims=True) + EPS) * nw_ref[...]
            h_ref[r, :] = (y * (1.0 + sc_ref[0]) + sh_ref[0]).astype(BF16)
            return carry

        lax.fori_loop(0, x_ref.shape[0] // NORM_ROWS, rows, 0)

    acc = jnp.dot(h_ref[...], w_ref[...], preferred_element_type=F32)
    for (start, ntiles), o_ref in zip(ranges, out_refs):
        @pl.when((j >= start) & (j < start + ntiles))
        def _(o_ref=o_ref):
            o_ref[...] = acc.astype(o_ref.dtype)


def norm_matmul(x, norm_w, shift, scale, w, outs, *, seq_len, tm=512, tn=1024):
    m, d = x.shape
    n = w.shape[1]
    nb = shift.shape[0]
    tm, tn = min(tm, m), min(tn, n)
    ranges, start = [], 0
    for width, _ in outs:
        ranges.append((start, width // tn))
        start += width // tn
    assert start * tn == n and m % tm == 0 and (nb == 1 or seq_len % tm == 0)

    def mod_map(i, j):
        return ((i * tm) // seq_len if nb > 1 else 0, 0, 0)

    def out_map(i, j, s, t):
        return (i, jnp.clip(j - s, 0, t - 1))

    return pl.pallas_call(
        functools.partial(_norm_mm_kernel, ranges=tuple(ranges)),
        grid=(m // tm, n // tn),
        in_specs=[pl.BlockSpec((tm, d), lambda i, j: (i, 0)),
                  pl.BlockSpec((1, d), lambda i, j: (0, 0)),
                  pl.BlockSpec((1, 1, d), mod_map),
                  pl.BlockSpec((1, 1, d), mod_map),
                  pl.BlockSpec((d, tn), lambda i, j: (0, j))],
        out_specs=[pl.BlockSpec((tm, tn), functools.partial(out_map, s=s, t=t)) for s, t in ranges],
        out_shape=[jax.ShapeDtypeStruct((m, width), dt) for width, dt in outs],
        scratch_shapes=[pltpu.VMEM((tm, d), BF16)],
        compiler_params=_params("parallel", "arbitrary"),
        name="norm_matmul",
    )(x, norm_w.reshape(1, d), shift, scale, w)


def _out_kernel(g_ref, w_ref, x_ref, gate_ref, nw_ref, *rest, final):
    acc = jnp.dot(g_ref[...], w_ref[...], preferred_element_type=F32)
    xn = x_ref[...] + gate_ref[0] * acc
    y = xn * lax.rsqrt(jnp.mean(xn * xn, axis=-1, keepdims=True) + EPS) * nw_ref[...]
    if final:
        (o_ref,) = rest
        o_ref[...] = y
    else:
        sh_ref, sc_ref, o_ref, h_ref = rest
        o_ref[...] = xn
        h_ref[...] = (y * (1.0 + sc_ref[0]) + sh_ref[0]).astype(BF16)


def out_residual(g, w, x, gate, norm_w, next_mod, *, seq_len, tm):
    m, k = g.shape
    d = w.shape[1]
    nb = gate.shape[0]
    final = next_mod is None
    assert m % tm == 0 and (nb == 1 or seq_len % tm == 0)

    def mod_map(i):
        return ((i * tm) // seq_len if nb > 1 else 0, 0, 0)

    row_spec = pl.BlockSpec((tm, d), lambda i: (i, 0))
    mod_spec = pl.BlockSpec((1, 1, d), mod_map)
    in_specs = [pl.BlockSpec((tm, k), lambda i: (i, 0)),
                pl.BlockSpec((k, d), lambda i: (0, 0)),
                row_spec, mod_spec,
                pl.BlockSpec((1, d), lambda i: (0, 0))]
    args = [g, w, x, gate, norm_w.reshape(1, d)]
    out_specs, out_shape = row_spec, jax.ShapeDtypeStruct((m, d), F32)
    if not final:
        in_specs += [mod_spec, mod_spec]
        args += list(next_mod)
        out_specs, out_shape = [row_spec, row_spec], [out_shape, jax.ShapeDtypeStruct((m, d), BF16)]
    return pl.pallas_call(
        functools.partial(_out_kernel, final=final),
        grid=(m // tm,),
        in_specs=in_specs,
        out_specs=out_specs,
        out_shape=out_shape,
        compiler_params=_params("parallel"),
        name="out_residual",
    )(*args)


def _mm_kernel(h_ref, w_ref, o_ref):
    o_ref[...] = jnp.dot(h_ref[...], w_ref[...], preferred_element_type=F32).astype(o_ref.dtype)


def matmul(h, w, out_dtype, *, tm=1024, tn=2048):
    m, k = h.shape
    n = w.shape[1]
    tm, tn = min(tm, m), min(tn, n)
    assert m % tm == 0 and n % tn == 0
    return pl.pallas_call(
        _mm_kernel,
        grid=(m // tm, n // tn),
        in_specs=[pl.BlockSpec((tm, k), lambda i, j: (i, 0)),
                  pl.BlockSpec((k, tn), lambda i, j: (0, j))],
        out_specs=pl.BlockSpec((tm, tn), lambda i, j: (i, j)),
        out_shape=jax.ShapeDtypeStruct((m, n), out_dtype),
        compiler_params=_params("parallel", "parallel"),
        name="matmul",
    )(h, w)


def _ctx_attn_kernel(q_ref, k_ref, v_ref, z_ref, o_ref, *, heads):
    scale = HEAD_DIM ** -0.5
    for h in range(heads):
        sl = slice(h * HEAD_DIM, (h + 1) * HEAD_DIM)
        q = q_ref[:, sl]
        k = k_ref[:, sl].astype(BF16)
        v = v_ref[:, sl].astype(BF16)
        s = lax.dot_general(q, k, NT_DIMS, preferred_element_type=F32) * scale
        p = jnp.exp(s - jnp.max(s, axis=-1, keepdims=True))
        o = jnp.dot(p.astype(BF16), v, preferred_element_type=F32) / jnp.sum(p, axis=-1, keepdims=True)
        o_ref[:, sl] = (o * _silu(z_ref[:, sl].astype(F32))).astype(BF16)


def context_attention(q, k, v, z, *, seq_len):
    m, width = q.shape
    spec = pl.BlockSpec((seq_len, width), lambda b: (b, 0))
    return pl.pallas_call(
        functools.partial(_ctx_attn_kernel, heads=width // HEAD_DIM),
        grid=(m // seq_len,),
        in_specs=[spec, spec, spec, spec],
        out_specs=spec,
        out_shape=jax.ShapeDtypeStruct((m, width), BF16),
        compiler_params=_params("parallel"),
        name="context_attention",
    )(q, k, v, z)


ATTN_Q_ROWS = 4
ATTN_GROUP = 2


def _attn_plan(rows):
    wr = min(WIN_R, rows)
    qb = min(ATTN_Q_ROWS, rows)
    kw = min(qb + wr - 1 + (qb + wr - 1) % 2, rows)
    assert rows % qb == 0 and kw % 2 == 0

    def window_start(r):
        return min(max(r - wr // 2, 0), rows - wr)

    blocks, patterns = [], []
    for r0 in range(0, rows, qb):
        k0 = min(max(r0 - wr // 2, 0), rows - kw)
        pat = tuple(tuple((k0 + i) - (r0 + j) + WIN_R - 1
                          if window_start(r0 + j) <= k0 + i < window_start(r0 + j) + wr else None
                          for i in range(kw)) for j in range(qb))
        assert all(sum(d is not None for d in row) == wr for row in pat)
        if pat not in patterns:
            patterns.append(pat)
        blocks.append((r0, k0, patterns.index(pat)))
    return qb, kw, blocks, patterns


def _bias_kernel(rpb_ref, o_ref, *, rows):
    h = pl.program_id(0)
    qb, kw, _, patterns = _attn_plan(rows)
    n_dr, n_dc = 2 * WIN_R - 1, 2 * WIN_C - 1
    lane = lax.broadcasted_iota(jnp.int32, (GRID_W, 2 * GRID_W), 1)
    qi = lax.broadcasted_iota(jnp.int32, (GRID_W, 2 * GRID_W), 0)
    kcol = lane & (GRID_W - 1)
    second = lane >= GRID_W
    dc = kcol - qi + (WIN_C - 1)
    col_start = jnp.clip(qi - WIN_C // 2, 0, GRID_W - WIN_C)
    col_in = (kcol >= col_start) & (kcol < col_start + WIN_C)
    cache = {}

    def tile(d_lo, d_hi):
        if (d_lo, d_hi) not in cache:
            acc = jnp.zeros((GRID_W, 2 * GRID_W), F32)
            ok = col_in
            if d_lo is None:
                ok = ok & second
            if d_hi is None:
                ok = ok & jnp.logical_not(second)
            if d_lo is not None or d_hi is not None:
                for c in range(n_dc):
                    lo = rpb_ref[(h * n_dr + d_lo) * n_dc + c] if d_lo is not None else 0.0
                    hi = rpb_ref[(h * n_dr + d_hi) * n_dc + c] if d_hi is not None else 0.0
                    acc = jnp.where(dc == c, jnp.where(second, hi, lo), acc)
            cache[(d_lo, d_hi)] = jnp.where(ok, acc, -jnp.inf)
        return cache[(d_lo, d_hi)]

    n_ctx = o_ref.shape[3] - kw * GRID_W
    for t, pat in enumerate(patterns):
        for j in range(qb):
            for i in range(0, kw, 2):
                o_ref[0, t, j * GRID_W:(j + 1) * GRID_W, i * GRID_W:(i + 2) * GRID_W] = tile(pat[j][i], pat[j][i + 1])
        o_ref[0, t, :, kw * GRID_W:] = jnp.zeros((qb * GRID_W, n_ctx), F32)


def window_bias(rpb, *, rows, ctx_len):
    heads = rpb.shape[0]
    qb, kw, _, patterns = _attn_plan(rows)
    shape = (len(patterns), qb * GRID_W, kw * GRID_W + ctx_len)
    return pl.pallas_call(
        functools.partial(_bias_kernel, rows=rows),
        grid=(heads,),
        in_specs=[pl.BlockSpec(memory_space=pltpu.SMEM)],
        out_specs=pl.BlockSpec((1,) + shape, lambda h: (h, 0, 0, 0)),
        out_shape=jax.ShapeDtypeStruct((heads,) + shape, F32),
        compiler_params=_params("parallel"),
        name="window_bias",
    )(rpb.reshape(-1))


def _lat_attn_kernel(q_ref, k_ref, v_ref, z_ref, ck_ref, cv_ref, tbl_ref, o_ref, *, rows):
    scale = HEAD_DIM ** -0.5
    qb, kw, blocks, _ = _attn_plan(rows)
    ck = ck_ref[0].astype(BF16)
    cv = cv_ref[0].astype(BF16)
    for g in range(0, len(blocks), ATTN_GROUP):
        grp = blocks[g:g + ATTN_GROUP]
        qrows = [slice(r0 * GRID_W, (r0 + qb) * GRID_W) for r0, _, _ in grp]
        krows = [slice(k0 * GRID_W, (k0 + kw) * GRID_W) for _, k0, _ in grp]
        s = [lax.dot_general(q_ref[qr, :], jnp.concatenate([k_ref[kr, :], ck], axis=0), NT_DIMS,
                             preferred_element_type=F32) * scale + tbl_ref[0, t]
             for qr, kr, (_, _, t) in zip(qrows, krows, grp)]
        p = [jnp.exp(x - jnp.max(x, axis=-1, keepdims=True)) for x in s]
        o = [jnp.dot(x.astype(BF16), jnp.concatenate([v_ref[kr, :], cv], axis=0), preferred_element_type=F32)
             / jnp.sum(x, axis=-1, keepdims=True) for x, kr in zip(p, krows)]
        for x, qr in zip(o, qrows):
            o_ref[qr, :] = (x * _silu(z_ref[qr, :].astype(F32))).astype(BF16)


def neighbourhood_attention(q, k, v, z, ck, cv, bias, *, seq_len):
    m, width = q.shape
    heads = width // HEAD_DIM
    rows = seq_len // GRID_W
    ctx_len = ck.shape[1]
    tok = pl.BlockSpec((seq_len, HEAD_DIM), lambda h, b: (b, h))
    ctx = pl.BlockSpec((1, ctx_len, HEAD_DIM), lambda h, b: (b, 0, h))
    return pl.pallas_call(
        functools.partial(_lat_attn_kernel, rows=rows),
        grid=(heads, m // seq_len),
        in_specs=[tok, tok, tok, tok, ctx, ctx,
                  pl.BlockSpec((1,) + bias.shape[1:], lambda h, b: (h, 0, 0, 0))],
        out_specs=tok,
        out_shape=jax.ShapeDtypeStruct((m, width), BF16),
        compiler_params=_params("parallel", "parallel"),
        name="neighbourhood_attention",
    )(q, k, v, z, ck, cv, bias)


CONV_PAD = 16
CONV_TILE_ELEMS = 1 << 18


def _conv_kernel(x_ref, w_ref, o_ref, pad_ref, *, l2_scale, rows_per_step):
    t = x_ref.shape[1]
    tc = x_ref.shape[2]
    rps = rows_per_step
    half = DN_CONV // 2
    win = rps + 2 * CONV_PAD
    pad_ref[0:CONV_PAD, :] = jnp.zeros((CONV_PAD, tc), BF16)
    pad_ref[CONV_PAD + t:2 * CONV_PAD + t, :] = jnp.zeros((CONV_PAD, tc), BF16)
    pad_ref[CONV_PAD:CONV_PAD + t, :] = x_ref[0]

    taps = [j for j in range(DN_CONV) if j != half]
    r = lax.broadcasted_iota(jnp.int32, (len(taps) * rps, win), 0)
    c = lax.broadcasted_iota(jnp.int32, (len(taps) * rps, win), 1)
    src = (r & (rps - 1)) + CONV_PAD - half
    for m, j in enumerate(taps):
        src = src + jnp.where(_div_pow2(r, rps) == m, j, 0)
    shift = jnp.where(c == src, 1.0, 0.0).astype(BF16)

    for r0 in range(0, t, rps):
        xw = pad_ref[r0:r0 + win, :]
        moved = jnp.dot(shift, xw, preferred_element_type=F32)
        acc = xw[CONV_PAD:CONV_PAD + rps].astype(F32) * w_ref[half:half + 1, :]
        for m, j in enumerate(taps):
            acc = acc + moved[m * rps:(m + 1) * rps] * w_ref[j:j + 1, :]
        y = _silu(acc)
        if l2_scale is not None:
            parts = []
            for g in range(tc // HEAD_DIM):
                yg = y[:, g * HEAD_DIM:(g + 1) * HEAD_DIM]
                parts.append(yg * (lax.rsqrt(jnp.sum(yg * yg, axis=-1, keepdims=True) + EPS) * l2_scale))
            y = jnp.concatenate(parts, axis=1)
        o_ref[0, r0:r0 + rows_per_step, :] = y.astype(BF16)


def conv_silu(x, conv_w, *, seq_len, col0, width, l2_scale, tc=512, rows_per_step=128):
    m, c_total = x.shape
    b = m // seq_len
    rows_per_step = min(rows_per_step, seq_len)
    tc = min(width, max(tc, CONV_TILE_ELEMS // seq_len))
    assert width % tc == 0 and col0 % tc == 0
    out = pl.pallas_call(
        functools.partial(_conv_kernel, l2_scale=l2_scale, rows_per_step=rows_per_step),
        grid=(b, width // tc),
        in_specs=[pl.BlockSpec((1, seq_len, tc), lambda i, j: (i, 0, col0 // tc + j)),
                  pl.BlockSpec((DN_CONV, tc), lambda i, j: (0, col0 // tc + j))],
        out_specs=pl.BlockSpec((1, seq_len, tc), lambda i, j: (i, 0, j)),
        out_shape=jax.ShapeDtypeStruct((b, seq_len, width), BF16),
        scratch_shapes=[pltpu.VMEM((seq_len + 2 * CONV_PAD, tc), BF16)],
        compiler_params=_params("parallel", "parallel"),
        name="conv_silu",
    )(x.reshape(b, seq_len, c_total), conv_w)
    return out.reshape(m, width)


def _gate_kernel(ba_ref, alog_ref, dtb_ref, o_ref):
    x = ba_ref[...]
    tm = x.shape[0]
    lane = lax.broadcasted_iota(jnp.int32, x.shape, 1)
    beta = jax.nn.sigmoid(x)
    xs = x + dtb_ref[...]
    softplus = jnp.maximum(xs, 0.0) + jnp.log1p(jnp.exp(-jnp.abs(xs)))
    g = jnp.where(lane >= 2 * DN_V_HEADS, -jnp.exp(alog_ref[...]) * softplus, 0.0)
    r = lax.broadcasted_iota(jnp.int32, (tm, tm), 0)
    c = lax.broadcasted_iota(jnp.int32, (tm, tm), 1)
    same = _div_pow2(r, DN_CHUNK) == _div_pow2(c, DN_CHUNK)
    lower = jnp.where(same & (c <= r), 1.0, 0.0).astype(BF16)
    upper = jnp.where(same & (c >= r), 1.0, 0.0).astype(BF16)
    dot = functools.partial(jnp.dot, preferred_element_type=F32)
    hi, mid, lo = _split3(g)
    cf = dot(lower, hi) + dot(lower, mid) + dot(lower, lo)
    cb = dot(upper, hi) + dot(upper, mid) + dot(upper, lo)
    gc = jnp.where(lane < 3 * DN_V_HEADS, cf, cb)
    o_ref[...] = jnp.where(lane < 2 * DN_V_HEADS, beta, gc)


def gate_sums(ba, a_log, dt_bias, *, tm=256):
    m, w = ba.shape
    zeros = jnp.zeros((2 * DN_V_HEADS,), F32)
    alog = jnp.concatenate([zeros, a_log.reshape(-1)]).reshape(1, w)
    dtb = jnp.concatenate([zeros, dt_bias.reshape(-1)]).reshape(1, w)
    return pl.pallas_call(
        _gate_kernel,
        grid=(m // tm,),
        in_specs=[pl.BlockSpec((tm, w), lambda i: (i, 0)),
                  pl.BlockSpec((1, w), lambda i: (0, 0)),
                  pl.BlockSpec((1, w), lambda i: (0, 0))],
        out_specs=pl.BlockSpec((tm, w), lambda i: (i, 0)),
        out_shape=jax.ShapeDtypeStruct((m, w), F32),
        compiler_params=_params("parallel"),
        name="gate_sums",
    )(ba, alog, dtb)


N_PROB = 4
PREPARE_GROUP = 8


def _block_diag(x, mask):
    xb = x.astype(BF16)
    return jnp.where(mask, jnp.concatenate([xb] * N_PROB, axis=0), jnp.zeros((), BF16))


def _unit_tri_inverse(a_list, row, col, eye, bd_mask):
    cs = a_list[0].shape[0]

    def same_block(b):
        return _div_pow2(row, b) == _div_pow2(col, b)

    def mm(xs, ys_bd):
        return [jnp.dot(x.astype(BF16), y, preferred_element_type=F32) for x, y in zip(xs, ys_bd)]

    def bd(xs):
        return [_block_diag(x, bd_mask) for x in xs]

    base = 8
    d = [jnp.where(same_block(base), a, 0.0) for a in a_list]
    d2 = mm(d, bd(d))
    d2_bd = bd(d2)
    d4_bd = bd(mm(d2, d2_bd))
    t = [eye - x for x in d]
    t = [x + y for x, y in zip(t, mm(t, d2_bd))]
    t = [x + y for x, y in zip(t, mm(t, d4_bd))]
    b = base
    while b < cs:
        join = same_block(2 * b) & jnp.logical_not(same_block(b))
        e_bd = bd([jnp.where(join, a, 0.0) for a in a_list])
        t = [x - y for x, y in zip(t, mm(mm(t, e_bd), bd(t)))]
        b *= 2
    return t


def _delta_kernel(*refs, n_chunks, hb, has_init, emit_state):
    q_ref, k_ref, v_ref, z_ref, pc_ref, pr_ref, nw_ref = refs[:7]
    refs = refs[7:]
    s0_ref = None
    if has_init:
        s0_ref, refs = refs[0], refs[1:]
    o_ref, refs = refs[0], refs[1:]
    sn_ref = None
    if emit_state:
        sn_ref, refs = refs[0], refs[1:]
    acc_ref, s_ref, n_ref, pm_ref = refs
    cs, hd = DN_CHUNK, HEAD_DIM
    wide = N_PROB * cs

    row = lax.broadcasted_iota(jnp.int32, (cs, wide), 0)
    lane_d = lax.broadcasted_iota(jnp.int32, (hd, wide), 1)
    lane = lax.broadcasted_iota(jnp.int32, (cs, wide), 1)
    col = lane & (cs - 1)
    fwd = lane < 2 * cs
    strict = (fwd & (col < row)) | (jnp.logical_not(fwd) & (col > row))
    incl = strict | (col == row)
    eye = jnp.where(col == row, 1.0, 0.0)
    first_half = (lax.broadcasted_iota(jnp.int32, (cs, 2 * cs), 1) < cs)
    bd_mask = (_div_pow2(lax.broadcasted_iota(jnp.int32, (wide, wide), 0), cs)
               == _div_pow2(lax.broadcasted_iota(jnp.int32, (wide, wide), 1), cs))

    def side_by_side(c0, c1, c2, c3):
        half = first_half[:c0.shape[0]]
        return jnp.concatenate([jnp.where(half, c0, c1), jnp.where(half, c2, c3)], axis=1)

    group = min(max(PREPARE_GROUP // hb, 1), n_chunks)
    units = [(cj, hh) for cj in range(group) for hh in range(hb)]
    each = range(len(units))
    head = [hh for _, hh in units]

    def prepare(gi, carry):
        cidx = [gi * group + cj for cj, _ in units]
        rows = [pl.ds(pl.multiple_of(c * cs, cs), cs) for c in cidx]
        qc = [q_ref[r, hh * hd:(hh + 1) * hd] for r, hh in zip(rows, head)]
        kc = [k_ref[r, hh * hd:(hh + 1) * hd] for r, hh in zip(rows, head)]
        kf = [x.astype(F32) for x in kc]
        pc = [pc_ref[0, hh, r, :] for r, hh in zip(rows, head)]
        gcr = [pr_ref[0, hh, c] for c, hh in zip(cidx, head)]
        beta = [[x[:, p:p + 1] for p in range(N_PROB)] for x in pc]
        gcc = [[x[:, N_PROB + p:N_PROB + p + 1] for p in range(N_PROB)] for x in pc]
        prod = [lax.dot_general(jnp.concatenate([qc[j], kc[j]], axis=0), jnp.concatenate([kc[j]] * N_PROB, axis=0),
                                NT_DIMS, preferred_element_type=F32) for j in each]
        decay = [jnp.exp(jnp.where(incl, side_by_side(*gcc[j]) - gcr[j], -jnp.inf)) for j in each]
        a = [jnp.where(strict, side_by_side(*beta[j]) * prod[j][cs:] * decay[j], 0.0) for j in each]
        t = _unit_tri_inverse(a, row, col, eye, bd_mask)
        rhs = []
        for j in each:
            parts = []
            for p in range(N_PROB):
                vcol = (2 * head[j] + p % 2) * hd
                vf = v_ref[rows[j], vcol:vcol + hd].astype(F32)
                parts.append(jnp.concatenate([vf * beta[j][p], kf[j] * (beta[j][p] * jnp.exp(gcc[j][p]))], axis=1))
            rhs.append(jnp.concatenate(parts, axis=0).astype(BF16))
        sol = [jnp.dot(_block_diag(t[j], bd_mask), rhs[j], preferred_element_type=F32).astype(BF16)
               for j in each]
        osol = [jnp.dot(_block_diag(prod[j][:cs] * decay[j], bd_mask), sol[j], preferred_element_type=F32)
                for j in each]
        ksol = []
        for j in each:
            g_last = [gcc[j][p][(cs - 1 if p < 2 else 0):(cs if p < 2 else 1), :] for p in range(N_PROB)]
            kdt = jnp.concatenate([kf[j]] * N_PROB, axis=0).T * jnp.exp(side_by_side(*g_last) - gcr[j])
            lhs = jnp.concatenate([jnp.where((lane_d >= p * cs) & (lane_d < (p + 1) * cs), kdt, 0.0)
                                   for p in range(N_PROB)], axis=0).astype(BF16)
            ksol.append(jnp.dot(lhs, sol[j], preferred_element_type=F32))
        for j in each:
            qf = qc[j].astype(F32)
            for p in range(N_PROB):
                kp = ksol[j][p * hd:(p + 1) * hd]
                op = osol[j][p * cs:(p + 1) * cs]
                n_ref[cidx[j], head[j] * N_PROB + p] = kp[:, :hd]
                pm_ref[cidx[j], head[j] * N_PROB + p, 0:hd] = (-kp[:, hd:]).astype(BF16)
                pm_ref[cidx[j], head[j] * N_PROB + p, hd:hd + cs] = (
                    qf * jnp.exp(gcc[j][p]) - op[:, hd:]).astype(BF16)
            for e in range(2):
                ocol = (2 * head[j] + e) * hd
                acc_ref[rows[j], ocol:ocol + hd] = (osol[j][e * cs:(e + 1) * cs, :hd]
                                                    + osol[j][(2 + e) * cs:(3 + e) * cs, :hd])
        return carry

    lax.fori_loop(0, n_chunks // group, prepare, 0)

    problems = [(hh, p) for hh in range(hb) for p in range(N_PROB)]
    for hh, p in problems:
        s_ref[hh * N_PROB + p] = s0_ref[0, p // 2, 2 * hh + p % 2] if has_init else jnp.zeros((hd, hd), F32)

    def advance(ci, carry):
        for hh, p in problems:
            ocol = (2 * hh + p % 2) * hd
            c = ci if p < 2 else n_chunks - 1 - ci
            last = cs - 1 if p < 2 else 0
            rows = pl.ds(pl.multiple_of(c * cs, cs), cs)
            g_last = pc_ref[0, hh, pl.ds(c * cs + last, 1), N_PROB + p:N_PROB + p + 1]
            s = s_ref[hh * N_PROB + p]
            ws = jnp.dot(pm_ref[c, hh * N_PROB + p], s.astype(BF16),
                         preferred_element_type=F32)
            s_ref[hh * N_PROB + p] = s * jnp.exp(g_last) + n_ref[c, hh * N_PROB + p] + ws[:hd]
            acc_ref[rows, ocol:ocol + hd] += ws[hd:]
        return carry

    lax.fori_loop(0, n_chunks, advance, 0)

    if emit_state:
        for hh, p in problems:
            sn_ref[0, 0, p // 2, 2 * hh + p % 2] = s_ref[hh * N_PROB + p]
    for e in range(2 * hb):
        sl = slice(e * HEAD_DIM, (e + 1) * HEAD_DIM)
        o = acc_ref[:, sl]
        y = o * lax.rsqrt(jnp.mean(o * o, axis=-1, keepdims=True) + EPS) * nw_ref[...]
        o_ref[:, sl] = (y * _silu(z_ref[:, sl].astype(F32))).astype(BF16)


def gated_delta(q, k, v, z, z_col0, p, norm_w, s0, *, seq_len, emit_state):
    m = q.shape[0]
    b = m // seq_len
    n_chunks = seq_len // DN_CHUNK
    rep = DN_V_HEADS // DN_K_HEADS
    assert 2 * rep == N_PROB
    p5 = p.reshape(b, seq_len, 2, 2, DN_K_HEADS, rep)
    pc = p5.transpose(0, 4, 1, 2, 3, 5).reshape(b, DN_K_HEADS, seq_len, 2 * N_PROB)
    pr = p5[:, :, 1].reshape(b, n_chunks, DN_CHUNK, 2, DN_K_HEADS, rep).transpose(0, 4, 1, 3, 5, 2).reshape(
        b, DN_K_HEADS, n_chunks, 1, N_PROB * DN_CHUNK)
    d = HEAD_DIM
    per_head = (n_chunks * N_PROB * (d * d * 4 + (d + DN_CHUNK) * d * 2) + seq_len * rep * d * 4
                + 2 * seq_len * (2 * d * 2 + 3 * rep * d * 2 + HEAD_DIM * 4))
    hb = max(h for h in (1, 2) if h == 1 or h * per_head <= VMEM_LIMIT // 2)
    z_block0 = z_col0 // (hb * rep * d)
    assert z_block0 * hb * rep * d == z_col0
    in_specs = [pl.BlockSpec((seq_len, hb * d), lambda i, h: (i, h)),
                pl.BlockSpec((seq_len, hb * d), lambda i, h: (i, h)),
                pl.BlockSpec((seq_len, hb * rep * d), lambda i, h: (i, h)),
                pl.BlockSpec((seq_len, hb * rep * d), lambda i, h: (i, z_block0 + h)),
                pl.BlockSpec((1, hb, seq_len, 2 * N_PROB), lambda i, h: (i, h, 0, 0)),
                pl.BlockSpec((1, hb, n_chunks, 1, N_PROB * DN_CHUNK), lambda i, h: (i, h, 0, 0, 0)),
                pl.BlockSpec((1, d), lambda i, h: (0, 0))]
    args = [q, k, v, z, pc, pr, norm_w.reshape(1, d)]
    if s0 is not None:
        in_specs.append(pl.BlockSpec((1, 2, hb * rep, d, d), lambda i, h: (i, 0, h, 0, 0)))
        args.append(s0)
    out_specs = [pl.BlockSpec((seq_len, hb * rep * d), lambda i, h: (i, h))]
    out_shape = [jax.ShapeDtypeStruct((m, DN_V_HEADS * d), BF16)]
    if emit_state:
        out_specs.append(pl.BlockSpec((1, 1, 2, hb * rep, d, d), lambda i, h: (i, 0, 0, h, 0, 0)))
        out_shape.append(jax.ShapeDtypeStruct((b, 1, 2, DN_V_HEADS, d, d), F32))
    res = pl.pallas_call(
        functools.partial(_delta_kernel, n_chunks=n_chunks, hb=hb, has_init=s0 is not None,
                          emit_state=emit_state),
        grid=(b, DN_K_HEADS // hb),
        in_specs=in_specs,
        out_specs=out_specs,
        out_shape=out_shape,
        scratch_shapes=[pltpu.VMEM((seq_len, hb * rep * d), F32),
                        pltpu.VMEM((hb * N_PROB, d, d), F32),
                        pltpu.VMEM((n_chunks, hb * N_PROB, d, d), F32),
                        pltpu.VMEM((n_chunks, hb * N_PROB, d + DN_CHUNK, d), BF16)],
        compiler_params=_params("parallel", "parallel"),
        name="gated_delta",
    )(*args)
    return res if emit_state else (res[0], None)


def _trunk(x, mod, rows, cache_k, cache_v, state, weights, *, emit):
    (norm_w, na_w_in, na_w_out, bias, dn_w_in, dn_w_ba, dn_conv_w, dn_a_log, dn_dt_bias,
     dn_norm_w, dn_w_out, final_norm_w) = weights
    b, t, d = x.shape
    x2 = x.reshape(b * t, d)
    kw = dict(seq_len=t)

    def mods(layer):
        mm = mod[layer, rows][:, None, :]
        return mm[..., :d], mm[..., d:2 * d], mm[..., 2 * d:]

    shift, scale, gate = mods(0)
    local = cache_k is not None
    kv_dt = BF16 if local else F32
    q, k, v, z = norm_matmul(x2, norm_w[0], shift, scale, na_w_in,
                             ((d, BF16), (d, kv_dt), (d, kv_dt), (d, BF16)), tm=1024 if local else 512, **kw)
    if local:
        g = neighbourhood_attention(q, k, v, z, cache_k, cache_v, bias, **kw)
    else:
        g = context_attention(q, k, v, z, **kw)
    shift, scale, gate1 = mods(1)
    x2, h1 = out_residual(g, na_w_out, x2, gate, norm_w[1], (shift, scale), tm=512, **kw)

    kw_dim, vw_dim = DN_K_HEADS * HEAD_DIM, DN_V_HEADS * HEAD_DIM
    proj = matmul(h1, dn_w_in, BF16)
    ba = matmul(h1, dn_w_ba, F32)
    qn = conv_silu(proj, dn_conv_w, col0=0, width=kw_dim, l2_scale=HEAD_DIM ** -0.5, **kw)
    kn = conv_silu(proj, dn_conv_w, col0=kw_dim, width=kw_dim, l2_scale=1.0, **kw)
    vn = conv_silu(proj, dn_conv_w, col0=2 * kw_dim, width=vw_dim, l2_scale=None, **kw)
    p = gate_sums(ba, dn_a_log, dn_dt_bias)
    og, s_new = gated_delta(qn, kn, vn, proj, 2 * kw_dim + vw_dim, p, dn_norm_w, state, emit_state=emit, **kw)
    y = out_residual(og, dn_w_out, x2, gate1, final_norm_w, None, tm=256, **kw)
    return y.reshape(b, t, d), k, v, s_new


def kernel(x_prompt, x_sample, cache_na_k, cache_na_v, state_dn, c, c_ctx, norm_w, ada_w, ada_b, na_w_in, na_w_out, na_rpb, dn_w_in, dn_conv_w, dn_a_log, dn_dt_bias, dn_norm_w, dn_w_out, final_norm_w):
    bp, tp, d = x_prompt.shape
    bs, ts, _ = x_sample.shape
    ctx_len = cache_na_k.shape[2]
    assert norm_w.shape[0] == 2 and na_w_in.shape[0] == 1 and dn_w_in.shape[0] == 1

    n_cond = -(-(bs + 1) // SUBLANES) * SUBLANES
    cond = jnp.concatenate([c, c_ctx[None], jnp.zeros((n_cond - bs - 1, d), F32)])
    mod = ada_mod(cond, ada_w, ada_b)

    conv_width = 2 * DN_K_HEADS * HEAD_DIM + DN_V_HEADS * HEAD_DIM
    n_main = conv_width + DN_V_HEADS * HEAD_DIM
    weights = (norm_w, na_w_in[0].astype(BF16), na_w_out[0].astype(BF16),
               window_bias(na_rpb[0], rows=ts // GRID_W, ctx_len=ctx_len),
               dn_w_in[0, :, :n_main].astype(BF16), dn_w_in[0, :, n_main:].astype(BF16),
               dn_conv_w[0], dn_a_log[0], dn_dt_bias[0], dn_norm_w[0], dn_w_out[0].astype(BF16),
               final_norm_w)

    y_prompt, k_new, v_new, s_new = _trunk(
        x_prompt, mod, slice(bs, bs + 1), None, None, None, weights, emit=True)
    y_sample, _, _, _ = _trunk(
        x_sample, mod, slice(0, bs), cache_na_k[:, 0].reshape(bs, ctx_len, d),
        cache_na_v[:, 0].reshape(bs, ctx_len, d), state_dn[:, 0], weights, emit=False)

    kv_shape = (bp, 1, tp, NA_HEADS, HEAD_DIM)
    return (y_prompt, y_sample, k_new.reshape(kv_shape), v_new.reshape(kv_shape), s_new)
```

```python
import functools

import jax
import jax.numpy as jnp
from jax import lax
from jax.experimental import pallas as pl
from jax.experimental.pallas import tpu as pltpu

F32 = jnp.float32
BF16 = jnp.bfloat16

EPS = 1e-6
GRID_W = 64
NA_HEADS = 16
HEAD_DIM = 128
WIN_R = 8
WIN_C = 16
DN_K_HEADS = 16
DN_V_HEADS = 32
DN_CONV = 5
DN_CHUNK = 64

V7X_VMEM_BYTES = 64 * 1024 * 1024
VMEM_LIMIT = V7X_VMEM_BYTES - 8 * 1024 * 1024
SUBLANES = 8

NT_DIMS = (((1,), (1,)), ((), ()))
TN_DIMS = (((0,), (0,)), ((), ()))


def _params(*sem):
    return pltpu.CompilerParams(dimension_semantics=sem, vmem_limit_bytes=VMEM_LIMIT)


def _silu(x):
    return x * jax.nn.sigmoid(x)


def _split3(x):
    hi = x.astype(BF16)
    r1 = x - hi.astype(F32)
    mid = r1.astype(BF16)
    lo = (r1 - mid.astype(F32)).astype(BF16)
    return hi, mid, lo


def _div_pow2(x, n):
    assert n & (n - 1) == 0
    return jnp.right_shift(x, n.bit_length() - 1)


def _mm(a, b):
    return jnp.dot(a.astype(BF16), b.astype(BF16), preferred_element_type=F32)


def _ada_kernel(cond_ref, w_ref, b_ref, o_ref):
    a_hi, a_mid, a_lo = _split3(_silu(cond_ref[...]))
    w_hi, w_mid, w_lo = _split3(w_ref[0])
    dot = functools.partial(jnp.dot, preferred_element_type=F32)
    acc = dot(a_hi, w_hi) + (dot(a_hi, w_mid) + dot(a_mid, w_hi)) + (
        dot(a_hi, w_lo) + dot(a_mid, w_mid) + dot(a_lo, w_hi))
    o_ref[0] = acc + b_ref[0]


def ada_mod(cond, ada_w, ada_b, *, tn=768):
    depth, d, n = ada_w.shape
    r = cond.shape[0]
    return pl.pallas_call(
        _ada_kernel,
        grid=(depth, n // tn),
        in_specs=[pl.BlockSpec((r, d), lambda i, j: (0, 0)),
                  pl.BlockSpec((1, d, tn), lambda i, j: (i, 0, j)),
                  pl.BlockSpec((1, 1, tn), lambda i, j: (i, 0, j))],
        out_specs=pl.BlockSpec((1, r, tn), lambda i, j: (i, 0, j)),
        out_shape=jax.ShapeDtypeStruct((depth, r, n), F32),
        compiler_params=_params("parallel", "parallel"),
        name="ada_mod",
    )(cond, ada_w, ada_b.reshape(depth, 1, n))


def _norm_mod_kernel(x_ref, nw_ref, sh_ref, sc_ref, h_ref):
    x = x_ref[...]
    y = x * lax.rsqrt(jnp.mean(x * x, axis=-1, keepdims=True) + EPS) * nw_ref[...]
    h_ref[...] = (y * (1.0 + sc_ref[0]) + sh_ref[0]).astype(BF16)


def norm_modulate(x, norm_w, shift, scale, *, seq_len, tm=256):
    m, d = x.shape
    nb = shift.shape[0]
    assert m % tm == 0 and (nb == 1 or seq_len % tm == 0)

    def mod_map(i):
        return ((i * tm) // seq_len if nb > 1 else 0, 0, 0)

    return pl.pallas_call(
        _norm_mod_kernel,
        grid=(m // tm,),
        in_specs=[pl.BlockSpec((tm, d), lambda i: (i, 0)),
                  pl.BlockSpec((1, d), lambda i: (0, 0)),
                  pl.BlockSpec((1, 1, d), mod_map),
                  pl.BlockSpec((1, 1, d), mod_map)],
        out_specs=pl.BlockSpec((tm, d), lambda i: (i, 0)),
        out_shape=jax.ShapeDtypeStruct((m, d), BF16),
        compiler_params=_params("parallel"),
        name="norm_modulate",
    )(x, norm_w.reshape(1, d), shift, scale)


def _out_kernel(g_ref, w_ref, x_ref, gate_ref, nw_ref, *rest, final):
    acc = jnp.dot(g_ref[...], w_ref[...], preferred_element_type=F32)
    xn = x_ref[...] + gate_ref[0] * acc
    y = xn * lax.rsqrt(jnp.mean(xn * xn, axis=-1, keepdims=True) + EPS) * nw_ref[...]
    if final:
        (o_ref,) = rest
        o_ref[...] = y
    else:
        sh_ref, sc_ref, o_ref, h_ref = rest
        o_ref[...] = xn
        h_ref[...] = (y * (1.0 + sc_ref[0]) + sh_ref[0]).astype(BF16)


def out_residual(g, w, x, gate, norm_w, next_mod, *, seq_len, tm):
    m, k = g.shape
    d = w.shape[1]
    nb = gate.shape[0]
    final = next_mod is None
    assert m % tm == 0 and (nb == 1 or seq_len % tm == 0)

    def mod_map(i):
        return ((i * tm) // seq_len if nb > 1 else 0, 0, 0)

    row_spec = pl.BlockSpec((tm, d), lambda i: (i, 0))
    mod_spec = pl.BlockSpec((1, 1, d), mod_map)
    in_specs = [pl.BlockSpec((tm, k), lambda i: (i, 0)),
                pl.BlockSpec((k, d), lambda i: (0, 0)),
                row_spec, mod_spec,
                pl.BlockSpec((1, d), lambda i: (0, 0))]
    args = [g, w, x, gate, norm_w.reshape(1, d)]
    out_specs, out_shape = row_spec, jax.ShapeDtypeStruct((m, d), F32)
    if not final:
        in_specs += [mod_spec, mod_spec]
        args += list(next_mod)
        out_specs, out_shape = [row_spec, row_spec], [out_shape, jax.ShapeDtypeStruct((m, d), BF16)]
    return pl.pallas_call(
        functools.partial(_out_kernel, final=final),
        grid=(m // tm,),
        in_specs=in_specs,
        out_specs=out_specs,
        out_shape=out_shape,
        compiler_params=_params("parallel"),
        name="out_residual",
    )(*args)


def _mm_kernel(h_ref, w_ref, o_ref):
    o_ref[...] = jnp.dot(h_ref[...], w_ref[...], preferred_element_type=F32).astype(o_ref.dtype)


def matmul(h, w, out_dtype, *, col0=0, width=None, tm=1024, tn=2048):
    m, k = h.shape
    n = w.shape[1] - col0 if width is None else width
    tm, tn = min(tm, m), min(tn, n)
    assert m % tm == 0 and n % tn == 0 and col0 % tn == 0
    return pl.pallas_call(
        _mm_kernel,
        grid=(m // tm, n // tn),
        in_specs=[pl.BlockSpec((tm, k), lambda i, j: (i, 0)),
                  pl.BlockSpec((k, tn), lambda i, j: (0, col0 // tn + j))],
        out_specs=pl.BlockSpec((tm, tn), lambda i, j: (i, j)),
        out_shape=jax.ShapeDtypeStruct((m, n), out_dtype),
        compiler_params=_params("parallel", "parallel"),
        name="matmul",
    )(h, w)


def _ctx_attn_kernel(q_ref, k_ref, v_ref, z_ref, o_ref, *, heads):
    scale = HEAD_DIM ** -0.5
    for h in range(heads):
        sl = slice(h * HEAD_DIM, (h + 1) * HEAD_DIM)
        q = q_ref[:, sl]
        k = k_ref[:, sl].astype(BF16)
        v = v_ref[:, sl].astype(BF16)
        s = lax.dot_general(q, k, NT_DIMS, preferred_element_type=F32) * scale
        p = jnp.exp(s - jnp.max(s, axis=-1, keepdims=True))
        o = jnp.dot(p.astype(BF16), v, preferred_element_type=F32) / jnp.sum(p, axis=-1, keepdims=True)
        o_ref[:, sl] = (o * _silu(z_ref[:, sl].astype(F32))).astype(BF16)


def context_attention(q, k, v, z, *, seq_len):
    m, width = q.shape
    spec = pl.BlockSpec((seq_len, width), lambda b: (b, 0))
    return pl.pallas_call(
        functools.partial(_ctx_attn_kernel, heads=width // HEAD_DIM),
        grid=(m // seq_len,),
        in_specs=[spec, spec, spec, spec],
        out_specs=spec,
        out_shape=jax.ShapeDtypeStruct((m, width), BF16),
        compiler_params=_params("parallel"),
        name="context_attention",
    )(q, k, v, z)


ATTN_Q_ROWS = 4
ATTN_GROUP = 2


def _attn_plan(rows):
    wr = min(WIN_R, rows)
    qb = min(ATTN_Q_ROWS, rows)
    kw = min(qb + wr - 1 + (qb + wr - 1) % 2, rows)
    assert rows % qb == 0 and kw % 2 == 0

    def window_start(r):
        return min(max(r - wr // 2, 0), rows - wr)

    blocks, patterns = [], []
    for r0 in range(0, rows, qb):
        k0 = min(max(r0 - wr // 2, 0), rows - kw)
        pat = tuple(tuple((k0 + i) - (r0 + j) + WIN_R - 1
                          if window_start(r0 + j) <= k0 + i < window_start(r0 + j) + wr else None
                          for i in range(kw)) for j in range(qb))
        assert all(sum(d is not None for d in row) == wr for row in pat)
        if pat not in patterns:
            patterns.append(pat)
        blocks.append((r0, k0, patterns.index(pat)))
    return qb, kw, blocks, patterns


def _bias_kernel(rpb_ref, o_ref, *, rows):
    h = pl.program_id(0)
    qb, kw, _, patterns = _attn_plan(rows)
    n_dr, n_dc = 2 * WIN_R - 1, 2 * WIN_C - 1
    lane = lax.broadcasted_iota(jnp.int32, (GRID_W, 2 * GRID_W), 1)
    qi = lax.broadcasted_iota(jnp.int32, (GRID_W, 2 * GRID_W), 0)
    kcol = lane & (GRID_W - 1)
    second = lane >= GRID_W
    dc = kcol - qi + (WIN_C - 1)
    col_start = jnp.clip(qi - WIN_C // 2, 0, GRID_W - WIN_C)
    col_in = (kcol >= col_start) & (kcol < col_start + WIN_C)
    cache = {}

    def tile(d_lo, d_hi):
        if (d_lo, d_hi) not in cache:
            acc = jnp.zeros((GRID_W, 2 * GRID_W), F32)
            ok = col_in
            if d_lo is None:
                ok = ok & second
            if d_hi is None:
                ok = ok & jnp.logical_not(second)
            if d_lo is not None or d_hi is not None:
                for c in range(n_dc):
                    lo = rpb_ref[(h * n_dr + d_lo) * n_dc + c] if d_lo is not None else 0.0
                    hi = rpb_ref[(h * n_dr + d_hi) * n_dc + c] if d_hi is not None else 0.0
                    acc = jnp.where(dc == c, jnp.where(second, hi, lo), acc)
            cache[(d_lo, d_hi)] = jnp.where(ok, acc, -jnp.inf)
        return cache[(d_lo, d_hi)]

    n_ctx = o_ref.shape[3] - kw * GRID_W
    for t, pat in enumerate(patterns):
        for j in range(qb):
            for i in range(0, kw, 2):
                o_ref[0, t, j * GRID_W:(j + 1) * GRID_W, i * GRID_W:(i + 2) * GRID_W] = tile(pat[j][i], pat[j][i + 1])
        o_ref[0, t, :, kw * GRID_W:] = jnp.zeros((qb * GRID_W, n_ctx), F32)


def window_bias(rpb, *, rows, ctx_len):
    heads = rpb.shape[0]
    qb, kw, _, patterns = _attn_plan(rows)
    shape = (len(patterns), qb * GRID_W, kw * GRID_W + ctx_len)
    return pl.pallas_call(
        functools.partial(_bias_kernel, rows=rows),
        grid=(heads,),
        in_specs=[pl.BlockSpec(memory_space=pltpu.SMEM)],
        out_specs=pl.BlockSpec((1,) + shape, lambda h: (h, 0, 0, 0)),
        out_shape=jax.ShapeDtypeStruct((heads,) + shape, F32),
        compiler_params=_params("parallel"),
        name="window_bias",
    )(rpb.reshape(-1))


def _lat_attn_kernel(q_ref, k_ref, v_ref, z_ref, ck_ref, cv_ref, tbl_ref, o_ref, *, rows):
    scale = HEAD_DIM ** -0.5
    qb, kw, blocks, _ = _attn_plan(rows)
    ck = ck_ref[0].astype(BF16)
    cv = cv_ref[0].astype(BF16)
    for g in range(0, len(blocks), ATTN_GROUP):
        grp = blocks[g:g + ATTN_GROUP]
        qrows = [slice(r0 * GRID_W, (r0 + qb) * GRID_W) for r0, _, _ in grp]
        krows = [slice(k0 * GRID_W, (k0 + kw) * GRID_W) for _, k0, _ in grp]
        s = [lax.dot_general(q_ref[qr, :], jnp.concatenate([k_ref[kr, :], ck], axis=0), NT_DIMS,
                             preferred_element_type=F32) * scale + tbl_ref[0, t]
             for qr, kr, (_, _, t) in zip(qrows, krows, grp)]
        p = [jnp.exp(x - jnp.max(x, axis=-1, keepdims=True)) for x in s]
        o = [jnp.dot(x.astype(BF16), jnp.concatenate([v_ref[kr, :], cv], axis=0), preferred_element_type=F32)
             / jnp.sum(x, axis=-1, keepdims=True) for x, kr in zip(p, krows)]
        for x, qr in zip(o, qrows):
            o_ref[qr, :] = (x * _silu(z_ref[qr, :].astype(F32))).astype(BF16)


def neighbourhood_attention(qkvz, ck, cv, bias, *, seq_len):
    m = qkvz.shape[0]
    heads = qkvz.shape[1] // (4 * HEAD_DIM)
    rows = seq_len // GRID_W
    ctx_len = ck.shape[1]
    tok = [pl.BlockSpec((seq_len, HEAD_DIM), functools.partial(lambda h, b, part: (b, part * heads + h), part=part))
           for part in range(4)]
    ctx = pl.BlockSpec((1, ctx_len, HEAD_DIM), lambda h, b: (b, 0, h))
    return pl.pallas_call(
        functools.partial(_lat_attn_kernel, rows=rows),
        grid=(heads, m // seq_len),
        in_specs=tok + [ctx, ctx, pl.BlockSpec((1,) + bias.shape[1:], lambda h, b: (h, 0, 0, 0))],
        out_specs=tok[0],
        out_shape=jax.ShapeDtypeStruct((m, heads * HEAD_DIM), BF16),
        compiler_params=_params("parallel", "parallel"),
        name="neighbourhood_attention",
    )(qkvz, qkvz, qkvz, qkvz, ck, cv, bias)


CONV_PAD = 16
CONV_TILE_ELEMS = 1 << 18


def _conv_kernel(x_ref, w_ref, o_ref, pad_ref, *, l2_scale, rows_per_step):
    t = x_ref.shape[1]
    tc = x_ref.shape[2]
    rps = rows_per_step
    half = DN_CONV // 2
    win = rps + 2 * CONV_PAD
    pad_ref[0:CONV_PAD, :] = jnp.zeros((CONV_PAD, tc), BF16)
    pad_ref[CONV_PAD + t:2 * CONV_PAD + t, :] = jnp.zeros((CONV_PAD, tc), BF16)
    pad_ref[CONV_PAD:CONV_PAD + t, :] = x_ref[0]

    taps = [j for j in range(DN_CONV) if j != half]
    r = lax.broadcasted_iota(jnp.int32, (len(taps) * rps, win), 0)
    c = lax.broadcasted_iota(jnp.int32, (len(taps) * rps, win), 1)
    src = (r & (rps - 1)) + CONV_PAD - half
    for m, j in enumerate(taps):
        src = src + jnp.where(_div_pow2(r, rps) == m, j, 0)
    shift = jnp.where(c == src, 1.0, 0.0).astype(BF16)

    for r0 in range(0, t, rps):
        xw = pad_ref[r0:r0 + win, :]
        moved = jnp.dot(shift, xw, preferred_element_type=F32)
        acc = xw[CONV_PAD:CONV_PAD + rps].astype(F32) * w_ref[half:half + 1, :]
        for m, j in enumerate(taps):
            acc = acc + moved[m * rps:(m + 1) * rps] * w_ref[j:j + 1, :]
        y = _silu(acc)
        if l2_scale is not None:
            parts = []
            for g in range(tc // HEAD_DIM):
                yg = y[:, g * HEAD_DIM:(g + 1) * HEAD_DIM]
                parts.append(yg * (lax.rsqrt(jnp.sum(yg * yg, axis=-1, keepdims=True) + EPS) * l2_scale))
            y = jnp.concatenate(parts, axis=1)
        o_ref[0, r0:r0 + rows_per_step, :] = y.astype(BF16)


def conv_silu(x, conv_w, *, seq_len, col0, width, l2_scale, tc=512, rows_per_step=128):
    m, c_total = x.shape
    b = m // seq_len
    rows_per_step = min(rows_per_step, seq_len)
    tc = min(width, max(tc, CONV_TILE_ELEMS // seq_len))
    assert width % tc == 0 and col0 % tc == 0
    out = pl.pallas_call(
        functools.partial(_conv_kernel, l2_scale=l2_scale, rows_per_step=rows_per_step),
        grid=(b, width // tc),
        in_specs=[pl.BlockSpec((1, seq_len, tc), lambda i, j: (i, 0, col0 // tc + j)),
                  pl.BlockSpec((DN_CONV, tc), lambda i, j: (0, col0 // tc + j))],
        out_specs=pl.BlockSpec((1, seq_len, tc), lambda i, j: (i, 0, j)),
        out_shape=jax.ShapeDtypeStruct((b, seq_len, width), BF16),
        scratch_shapes=[pltpu.VMEM((seq_len + 2 * CONV_PAD, tc), BF16)],
        compiler_params=_params("parallel", "parallel"),
        name="conv_silu",
    )(x.reshape(b, seq_len, c_total), conv_w)
    return out.reshape(m, width)


def _gate_kernel(ba_ref, alog_ref, dtb_ref, o_ref):
    x = ba_ref[...]
    tm = x.shape[0]
    lane = lax.broadcasted_iota(jnp.int32, x.shape, 1)
    beta = jax.nn.sigmoid(x)
    xs = x + dtb_ref[...]
    softplus = jnp.maximum(xs, 0.0) + jnp.log1p(jnp.exp(-jnp.abs(xs)))
    g = jnp.where(lane >= 2 * DN_V_HEADS, -jnp.exp(alog_ref[...]) * softplus, 0.0)
    r = lax.broadcasted_iota(jnp.int32, (tm, tm), 0)
    c = lax.broadcasted_iota(jnp.int32, (tm, tm), 1)
    same = _div_pow2(r, DN_CHUNK) == _div_pow2(c, DN_CHUNK)
    lower = jnp.where(same & (c <= r), 1.0, 0.0).astype(BF16)
    upper = jnp.where(same & (c >= r), 1.0, 0.0).astype(BF16)
    dot = functools.partial(jnp.dot, preferred_element_type=F32)
    hi, mid, lo = _split3(g)
    cf = dot(lower, hi) + dot(lower, mid) + dot(lower, lo)
    cb = dot(upper, hi) + dot(upper, mid) + dot(upper, lo)
    gc = jnp.where(lane < 3 * DN_V_HEADS, cf, cb)
    o_ref[...] = jnp.where(lane < 2 * DN_V_HEADS, beta, gc)


def gate_sums(ba, a_log, dt_bias, *, tm=256):
    m, w = ba.shape
    zeros = jnp.zeros((2 * DN_V_HEADS,), F32)
    alog = jnp.concatenate([zeros, a_log.reshape(-1)]).reshape(1, w)
    dtb = jnp.concatenate([zeros, dt_bias.reshape(-1)]).reshape(1, w)
    return pl.pallas_call(
        _gate_kernel,
        grid=(m // tm,),
        in_specs=[pl.BlockSpec((tm, w), lambda i: (i, 0)),
                  pl.BlockSpec((1, w), lambda i: (0, 0)),
                  pl.BlockSpec((1, w), lambda i: (0, 0))],
        out_specs=pl.BlockSpec((tm, w), lambda i: (i, 0)),
        out_shape=jax.ShapeDtypeStruct((m, w), F32),
        compiler_params=_params("parallel"),
        name="gate_sums",
    )(ba, alog, dtb)


N_PROB = 4
PREPARE_GROUP = 8


def _block_diag(x, mask):
    xb = x.astype(BF16)
    return jnp.where(mask, jnp.concatenate([xb] * N_PROB, axis=0), jnp.zeros((), BF16))


def _unit_tri_inverse(a_list, row, col, eye, bd_mask):
    cs = a_list[0].shape[0]

    def same_block(b):
        return _div_pow2(row, b) == _div_pow2(col, b)

    def mm(xs, ys_bd):
        return [jnp.dot(x.astype(BF16), y, preferred_element_type=F32) for x, y in zip(xs, ys_bd)]

    def bd(xs):
        return [_block_diag(x, bd_mask) for x in xs]

    base = 8
    d = [jnp.where(same_block(base), a, 0.0) for a in a_list]
    d2 = mm(d, bd(d))
    d2_bd = bd(d2)
    d4_bd = bd(mm(d2, d2_bd))
    t = [eye - x for x in d]
    t = [x + y for x, y in zip(t, mm(t, d2_bd))]
    t = [x + y for x, y in zip(t, mm(t, d4_bd))]
    b = base
    while b < cs:
        join = same_block(2 * b) & jnp.logical_not(same_block(b))
        e_bd = bd([jnp.where(join, a, 0.0) for a in a_list])
        t = [x - y for x, y in zip(t, mm(mm(t, e_bd), bd(t)))]
        b *= 2
    return t


def _delta_kernel(*refs, n_chunks, hb, has_init, emit_state):
    q_ref, k_ref, v_ref, z_ref, pc_ref, pr_ref, nw_ref = refs[:7]
    refs = refs[7:]
    s0_ref = None
    if has_init:
        s0_ref, refs = refs[0], refs[1:]
    o_ref, refs = refs[0], refs[1:]
    sn_ref = None
    if emit_state:
        sn_ref, refs = refs[0], refs[1:]
    acc_ref, s_ref, n_ref, pm_ref = refs
    cs, hd = DN_CHUNK, HEAD_DIM
    wide = N_PROB * cs

    row = lax.broadcasted_iota(jnp.int32, (cs, wide), 0)
    lane_d = lax.broadcasted_iota(jnp.int32, (hd, wide), 1)
    lane = lax.broadcasted_iota(jnp.int32, (cs, wide), 1)
    col = lane & (cs - 1)
    fwd = lane < 2 * cs
    strict = (fwd & (col < row)) | (jnp.logical_not(fwd) & (col > row))
    incl = strict | (col == row)
    eye = jnp.where(col == row, 1.0, 0.0)
    first_half = (lax.broadcasted_iota(jnp.int32, (cs, 2 * cs), 1) < cs)
    bd_mask = (_div_pow2(lax.broadcasted_iota(jnp.int32, (wide, wide), 0), cs)
               == _div_pow2(lax.broadcasted_iota(jnp.int32, (wide, wide), 1), cs))

    def side_by_side(c0, c1, c2, c3):
        half = first_half[:c0.shape[0]]
        return jnp.concatenate([jnp.where(half, c0, c1), jnp.where(half, c2, c3)], axis=1)

    group = min(max(PREPARE_GROUP // hb, 1), n_chunks)
    units = [(cj, hh) for cj in range(group) for hh in range(hb)]
    each = range(len(units))
    head = [hh for _, hh in units]

    def prepare(gi, carry):
        cidx = [gi * group + cj for cj, _ in units]
        rows = [pl.ds(pl.multiple_of(c * cs, cs), cs) for c in cidx]
        qc = [q_ref[r, hh * hd:(hh + 1) * hd] for r, hh in zip(rows, head)]
        kc = [k_ref[r, hh * hd:(hh + 1) * hd] for r, hh in zip(rows, head)]
        kf = [x.astype(F32) for x in kc]
        pc = [pc_ref[0, hh, r, :] for r, hh in zip(rows, head)]
        gcr = [pr_ref[0, hh, c] for c, hh in zip(cidx, head)]
        beta = [[x[:, p:p + 1] for p in range(N_PROB)] for x in pc]
        gcc = [[x[:, N_PROB + p:N_PROB + p + 1] for p in range(N_PROB)] for x in pc]
        prod = [lax.dot_general(jnp.concatenate([qc[j], kc[j]], axis=0), jnp.concatenate([kc[j]] * N_PROB, axis=0),
                                NT_DIMS, preferred_element_type=F32) for j in each]
        decay = [jnp.exp(jnp.where(incl, side_by_side(*gcc[j]) - gcr[j], -jnp.inf)) for j in each]
        a = [jnp.where(strict, side_by_side(*beta[j]) * prod[j][cs:] * decay[j], 0.0) for j in each]
        t = _unit_tri_inverse(a, row, col, eye, bd_mask)
        rhs = []
        for j in each:
            parts = []
            for p in range(N_PROB):
                vcol = (2 * head[j] + p % 2) * hd
                vf = v_ref[rows[j], vcol:vcol + hd].astype(F32)
                parts.append(jnp.concatenate([vf * beta[j][p], kf[j] * (beta[j][p] * jnp.exp(gcc[j][p]))], axis=1))
            rhs.append(jnp.concatenate(parts, axis=0).astype(BF16))
        sol = [jnp.dot(_block_diag(t[j], bd_mask), rhs[j], preferred_element_type=F32).astype(BF16)
               for j in each]
        osol = [jnp.dot(_block_diag(prod[j][:cs] * decay[j], bd_mask), sol[j], preferred_element_type=F32)
                for j in each]
        ksol = []
        for j in each:
            g_last = [gcc[j][p][(cs - 1 if p < 2 else 0):(cs if p < 2 else 1), :] for p in range(N_PROB)]
            kdt = jnp.concatenate([kf[j]] * N_PROB, axis=0).T * jnp.exp(side_by_side(*g_last) - gcr[j])
            lhs = jnp.concatenate([jnp.where((lane_d >= p * cs) & (lane_d < (p + 1) * cs), kdt, 0.0)
                                   for p in range(N_PROB)], axis=0).astype(BF16)
            ksol.append(jnp.dot(lhs, sol[j], preferred_element_type=F32))
        for j in each:
            qf = qc[j].astype(F32)
            for p in range(N_PROB):
                kp = ksol[j][p * hd:(p + 1) * hd]
                op = osol[j][p * cs:(p + 1) * cs]
                n_ref[cidx[j], head[j] * N_PROB + p] = kp[:, :hd]
                pm_ref[cidx[j], head[j] * N_PROB + p, 0:hd] = (-kp[:, hd:]).astype(BF16)
                pm_ref[cidx[j], head[j] * N_PROB + p, hd:hd + cs] = (
                    qf * jnp.exp(gcc[j][p]) - op[:, hd:]).astype(BF16)
            for e in range(2):
                ocol = (2 * head[j] + e) * hd
                acc_ref[rows[j], ocol:ocol + hd] = (osol[j][e * cs:(e + 1) * cs, :hd]
                                                    + osol[j][(2 + e) * cs:(3 + e) * cs, :hd])
        return carry

    lax.fori_loop(0, n_chunks // group, prepare, 0)

    problems = [(hh, p) for hh in range(hb) for p in range(N_PROB)]
    for hh, p in problems:
        s_ref[hh * N_PROB + p] = s0_ref[0, p // 2, 2 * hh + p % 2] if has_init else jnp.zeros((hd, hd), F32)

    def advance(ci, carry):
        for hh, p in problems:
            ocol = (2 * hh + p % 2) * hd
            c = ci if p < 2 else n_chunks - 1 - ci
            last = cs - 1 if p < 2 else 0
            rows = pl.ds(pl.multiple_of(c * cs, cs), cs)
            g_last = pc_ref[0, hh, pl.ds(c * cs + last, 1), N_PROB + p:N_PROB + p + 1]
            s = s_ref[hh * N_PROB + p]
            ws = jnp.dot(pm_ref[c, hh * N_PROB + p], s.astype(BF16),
                         preferred_element_type=F32)
            s_ref[hh * N_PROB + p] = s * jnp.exp(g_last) + n_ref[c, hh * N_PROB + p] + ws[:hd]
            acc_ref[rows, ocol:ocol + hd] += ws[hd:]
        return carry

    lax.fori_loop(0, n_chunks, advance, 0)

    if emit_state:
        for hh, p in problems:
            sn_ref[0, 0, p // 2, 2 * hh + p % 2] = s_ref[hh * N_PROB + p]
    for e in range(2 * hb):
        sl = slice(e * HEAD_DIM, (e + 1) * HEAD_DIM)
        o = acc_ref[:, sl]
        y = o * lax.rsqrt(jnp.mean(o * o, axis=-1, keepdims=True) + EPS) * nw_ref[...]
        o_ref[:, sl] = (y * _silu(z_ref[:, sl].astype(F32))).astype(BF16)


def gated_delta(q, k, v, z, z_col0, p, norm_w, s0, *, seq_len, emit_state):
    m = q.shape[0]
    b = m // seq_len
    n_chunks = seq_len // DN_CHUNK
    rep = DN_V_HEADS // DN_K_HEADS
    assert 2 * rep == N_PROB
    p5 = p.reshape(b, seq_len, 2, 2, DN_K_HEADS, rep)
    pc = p5.transpose(0, 4, 1, 2, 3, 5).reshape(b, DN_K_HEADS, seq_len, 2 * N_PROB)
    pr = p5[:, :, 1].reshape(b, n_chunks, DN_CHUNK, 2, DN_K_HEADS, rep).transpose(0, 4, 1, 3, 5, 2).reshape(
        b, DN_K_HEADS, n_chunks, 1, N_PROB * DN_CHUNK)
    d = HEAD_DIM
    per_head = (n_chunks * N_PROB * (d * d * 4 + (d + DN_CHUNK) * d * 2) + seq_len * rep * d * 4
                + 2 * seq_len * (2 * d * 2 + 3 * rep * d * 2 + HEAD_DIM * 4))
    hb = max(h for h in (1, 2, 4) if h == 1 or h * per_head <= VMEM_LIMIT // 2)
    z_block0 = z_col0 // (hb * rep * d)
    assert z_block0 * hb * rep * d == z_col0
    in_specs = [pl.BlockSpec((seq_len, hb * d), lambda i, h: (i, h)),
                pl.BlockSpec((seq_len, hb * d), lambda i, h: (i, h)),
                pl.BlockSpec((seq_len, hb * rep * d), lambda i, h: (i, h)),
                pl.BlockSpec((seq_len, hb * rep * d), lambda i, h: (i, z_block0 + h)),
                pl.BlockSpec((1, hb, seq_len, 2 * N_PROB), lambda i, h: (i, h, 0, 0)),
                pl.BlockSpec((1, hb, n_chunks, 1, N_PROB * DN_CHUNK), lambda i, h: (i, h, 0, 0, 0)),
                pl.BlockSpec((1, d), lambda i, h: (0, 0))]
    args = [q, k, v, z, pc, pr, norm_w.reshape(1, d)]
    if s0 is not None:
        in_specs.append(pl.BlockSpec((1, 2, hb * rep, d, d), lambda i, h: (i, 0, h, 0, 0)))
        args.append(s0)
    out_specs = [pl.BlockSpec((seq_len, hb * rep * d), lambda i, h: (i, h))]
    out_shape = [jax.ShapeDtypeStruct((m, DN_V_HEADS * d), BF16)]
    if emit_state:
        out_specs.append(pl.BlockSpec((1, 1, 2, hb * rep, d, d), lambda i, h: (i, 0, 0, h, 0, 0)))
        out_shape.append(jax.ShapeDtypeStruct((b, 1, 2, DN_V_HEADS, d, d), F32))
    res = pl.pallas_call(
        functools.partial(_delta_kernel, n_chunks=n_chunks, hb=hb, has_init=s0 is not None,
                          emit_state=emit_state),
        grid=(b, DN_K_HEADS // hb),
        in_specs=in_specs,
        out_specs=out_specs,
        out_shape=out_shape,
        scratch_shapes=[pltpu.VMEM((seq_len, hb * rep * d), F32),
                        pltpu.VMEM((hb * N_PROB, d, d), F32),
                        pltpu.VMEM((n_chunks, hb * N_PROB, d, d), F32),
                        pltpu.VMEM((n_chunks, hb * N_PROB, d + DN_CHUNK, d), BF16)],
        compiler_params=_params("parallel", "parallel"),
        name="gated_delta",
    )(*args)
    return res if emit_state else (res[0], None)


def _trunk(x, mod, rows, cache_k, cache_v, state, weights, *, emit):
    (norm_w, na_w_in, na_w_out, bias, dn_w_in, dn_w_ba, dn_conv_w, dn_a_log, dn_dt_bias,
     dn_norm_w, dn_w_out, final_norm_w) = weights
    b, t, d = x.shape
    x2 = x.reshape(b * t, d)
    kw = dict(seq_len=t)

    def mods(layer):
        mm = mod[layer, rows][:, None, :]
        return mm[..., :d], mm[..., d:2 * d], mm[..., 2 * d:]

    shift, scale, gate = mods(0)
    h0 = norm_modulate(x2, norm_w[0], shift, scale, **kw)
    if cache_k is not None:
        k = v = None
        g = neighbourhood_attention(matmul(h0, na_w_in, BF16), cache_k, cache_v, bias, **kw)
    else:
        q = matmul(h0, na_w_in, BF16, col0=0, width=d)
        k = matmul(h0, na_w_in, F32, col0=d, width=d)
        v = matmul(h0, na_w_in, F32, col0=2 * d, width=d)
        z = matmul(h0, na_w_in, BF16, col0=3 * d, width=d)
        g = context_attention(q, k, v, z, **kw)
    shift, scale, gate1 = mods(1)
    x2, h1 = out_residual(g, na_w_out, x2, gate, norm_w[1], (shift, scale), tm=512, **kw)

    kw_dim, vw_dim = DN_K_HEADS * HEAD_DIM, DN_V_HEADS * HEAD_DIM
    proj = matmul(h1, dn_w_in, BF16)
    ba = matmul(h1, dn_w_ba, F32)
    qn = conv_silu(proj, dn_conv_w, col0=0, width=kw_dim, l2_scale=HEAD_DIM ** -0.5, **kw)
    kn = conv_silu(proj, dn_conv_w, col0=kw_dim, width=kw_dim, l2_scale=1.0, **kw)
    vn = conv_silu(proj, dn_conv_w, col0=2 * kw_dim, width=vw_dim, l2_scale=None, **kw)
    p = gate_sums(ba, dn_a_log, dn_dt_bias)
    og, s_new = gated_delta(qn, kn, vn, proj, 2 * kw_dim + vw_dim, p, dn_norm_w, state, emit_state=emit, **kw)
    y = out_residual(og, dn_w_out, x2, gate1, final_norm_w, None, tm=256, **kw)
    return y.reshape(b, t, d), k, v, s_new


def kernel(x_prompt, x_sample, cache_na_k, cache_na_v, state_dn, c, c_ctx, norm_w, ada_w, ada_b, na_w_in, na_w_out, na_rpb, dn_w_in, dn_conv_w, dn_a_log, dn_dt_bias, dn_norm_w, dn_w_out, final_norm_w):
    bp, tp, d = x_prompt.shape
    bs, ts, _ = x_sample.shape
    ctx_len = cache_na_k.shape[2]
    assert norm_w.shape[0] == 2 and na_w_in.shape[0] == 1 and dn_w_in.shape[0] == 1

    n_cond = -(-(bs + 1) // SUBLANES) * SUBLANES
    cond = jnp.concatenate([c, c_ctx[None], jnp.zeros((n_cond - bs - 1, d), F32)])
    mod = ada_mod(cond, ada_w, ada_b)

    conv_width = 2 * DN_K_HEADS * HEAD_DIM + DN_V_HEADS * HEAD_DIM
    n_main = conv_width + DN_V_HEADS * HEAD_DIM
    weights = (norm_w, na_w_in[0].astype(BF16), na_w_out[0].astype(BF16),
               window_bias(na_rpb[0], rows=ts // GRID_W, ctx_len=ctx_len),
               dn_w_in[0, :, :n_main].astype(BF16), dn_w_in[0, :, n_main:].astype(BF16),
               dn_conv_w[0], dn_a_log[0], dn_dt_bias[0], dn_norm_w[0], dn_w_out[0].astype(BF16),
               final_norm_w)

    y_prompt, k_new, v_new, s_new = _trunk(
        x_prompt, mod, slice(bs, bs + 1), None, None, None, weights, emit=True)
    y_sample, _, _, _ = _trunk(
        x_sample, mod, slice(0, bs), cache_na_k[:, 0].reshape(bs, ctx_len, d),
        cache_na_v[:, 0].reshape(bs, ctx_len, d), state_dn[:, 0], weights, emit=False)

    kv_shape = (bp, 1, tp, NA_HEADS, HEAD_DIM)
    return (y_prompt, y_sample, k_new.reshape(kv_shape), v_new.reshape(kv_shape), s_new)
```

```python
import functools

import jax
import jax.numpy as jnp
from jax import lax
from jax.experimental import pallas as pl
from jax.experimental.pallas import tpu as pltpu

F32 = jnp.float32
BF16 = jnp.bfloat16

EPS = 1e-6
GRID_W = 64
NA_HEADS = 16
HEAD_DIM = 128
WIN_R = 8
WIN_C = 16
DN_K_HEADS = 16
DN_V_HEADS = 32
DN_CONV = 5
DN_CHUNK = 64

V7X_VMEM_BYTES = 64 * 1024 * 1024
VMEM_LIMIT = V7X_VMEM_BYTES - 8 * 1024 * 1024
SUBLANES = 8

NT_DIMS = (((1,), (1,)), ((), ()))
TN_DIMS = (((0,), (0,)), ((), ()))


def _params(*sem):
    return pltpu.CompilerParams(dimension_semantics=sem, vmem_limit_bytes=VMEM_LIMIT)


def _silu(x):
    return x * jax.nn.sigmoid(x)


def _split3(x):
    hi = x.astype(BF16)
    r1 = x - hi.astype(F32)
    mid = r1.astype(BF16)
    lo = (r1 - mid.astype(F32)).astype(BF16)
    return hi, mid, lo


def _div_pow2(x, n):
    assert n & (n - 1) == 0
    return jnp.right_shift(x, n.bit_length() - 1)


def _mm(a, b):
    return jnp.dot(a.astype(BF16), b.astype(BF16), preferred_element_type=F32)


def _ada_kernel(cond_ref, w_ref, b_ref, o_ref):
    a_hi, a_mid, a_lo = _split3(_silu(cond_ref[...]))
    w_hi, w_mid, w_lo = _split3(w_ref[0])
    dot = functools.partial(jnp.dot, preferred_element_type=F32)
    acc = dot(a_hi, w_hi) + (dot(a_hi, w_mid) + dot(a_mid, w_hi)) + (
        dot(a_hi, w_lo) + dot(a_mid, w_mid) + dot(a_lo, w_hi))
    o_ref[0] = acc + b_ref[0]


def ada_mod(cond, ada_w, ada_b, *, tn=768):
    depth, d, n = ada_w.shape
    r = cond.shape[0]
    return pl.pallas_call(
        _ada_kernel,
        grid=(depth, n // tn),
        in_specs=[pl.BlockSpec((r, d), lambda i, j: (0, 0)),
                  pl.BlockSpec((1, d, tn), lambda i, j: (i, 0, j)),
                  pl.BlockSpec((1, 1, tn), lambda i, j: (i, 0, j))],
        out_specs=pl.BlockSpec((1, r, tn), lambda i, j: (i, 0, j)),
        out_shape=jax.ShapeDtypeStruct((depth, r, n), F32),
        compiler_params=_params("parallel", "parallel"),
        name="ada_mod",
    )(cond, ada_w, ada_b.reshape(depth, 1, n))


def _norm_mod_kernel(x_ref, nw_ref, sh_ref, sc_ref, h_ref):
    x = x_ref[...]
    y = x * lax.rsqrt(jnp.mean(x * x, axis=-1, keepdims=True) + EPS) * nw_ref[...]
    h_ref[...] = (y * (1.0 + sc_ref[0]) + sh_ref[0]).astype(BF16)


def norm_modulate(x, norm_w, shift, scale, *, seq_len, tm=256):
    m, d = x.shape
    nb = shift.shape[0]
    assert m % tm == 0 and (nb == 1 or seq_len % tm == 0)

    def mod_map(i):
        return ((i * tm) // seq_len if nb > 1 else 0, 0, 0)

    return pl.pallas_call(
        _norm_mod_kernel,
        grid=(m // tm,),
        in_specs=[pl.BlockSpec((tm, d), lambda i: (i, 0)),
                  pl.BlockSpec((1, d), lambda i: (0, 0)),
                  pl.BlockSpec((1, 1, d), mod_map),
                  pl.BlockSpec((1, 1, d), mod_map)],
        out_specs=pl.BlockSpec((tm, d), lambda i: (i, 0)),
        out_shape=jax.ShapeDtypeStruct((m, d), BF16),
        compiler_params=_params("parallel"),
        name="norm_modulate",
    )(x, norm_w.reshape(1, d), shift, scale)


def _out_kernel(g_ref, w_ref, x_ref, gate_ref, nw_ref, *rest, final):
    acc = jnp.dot(g_ref[...], w_ref[...], preferred_element_type=F32)
    xn = x_ref[...] + gate_ref[0] * acc
    y = xn * lax.rsqrt(jnp.mean(xn * xn, axis=-1, keepdims=True) + EPS) * nw_ref[...]
    if final:
        (o_ref,) = rest
        o_ref[...] = y
    else:
        sh_ref, sc_ref, o_ref, h_ref = rest
        o_ref[...] = xn
        h_ref[...] = (y * (1.0 + sc_ref[0]) + sh_ref[0]).astype(BF16)


def out_residual(g, w, x, gate, norm_w, next_mod, *, seq_len, tm):
    m, k = g.shape
    d = w.shape[1]
    nb = gate.shape[0]
    final = next_mod is None
    assert m % tm == 0 and (nb == 1 or seq_len % tm == 0)

    def mod_map(i):
        return ((i * tm) // seq_len if nb > 1 else 0, 0, 0)

    row_spec = pl.BlockSpec((tm, d), lambda i: (i, 0))
    mod_spec = pl.BlockSpec((1, 1, d), mod_map)
    in_specs = [pl.BlockSpec((tm, k), lambda i: (i, 0)),
                pl.BlockSpec((k, d), lambda i: (0, 0)),
                row_spec, mod_spec,
                pl.BlockSpec((1, d), lambda i: (0, 0))]
    args = [g, w, x, gate, norm_w.reshape(1, d)]
    out_specs, out_shape = row_spec, jax.ShapeDtypeStruct((m, d), F32)
    if not final:
        in_specs += [mod_spec, mod_spec]
        args += list(next_mod)
        out_specs, out_shape = [row_spec, row_spec], [out_shape, jax.ShapeDtypeStruct((m, d), BF16)]
    return pl.pallas_call(
        functools.partial(_out_kernel, final=final),
        grid=(m // tm,),
        in_specs=in_specs,
        out_specs=out_specs,
        out_shape=out_shape,
        compiler_params=_params("parallel"),
        name="out_residual",
    )(*args)


def _mm_kernel(h_ref, w_ref, o_ref):
    o_ref[...] = jnp.dot(h_ref[...], w_ref[...], preferred_element_type=F32).astype(o_ref.dtype)


def matmul(h, w, out_dtype, *, col0=0, width=None, tm=1024, tn=2048):
    m, k = h.shape
    n = w.shape[1] - col0 if width is None else width
    tm, tn = min(tm, m), min(tn, n)
    assert m % tm == 0 and n % tn == 0 and col0 % tn == 0
    return pl.pallas_call(
        _mm_kernel,
        grid=(m // tm, n // tn),
        in_specs=[pl.BlockSpec((tm, k), lambda i, j: (i, 0)),
                  pl.BlockSpec((k, tn), lambda i, j: (0, col0 // tn + j))],
        out_specs=pl.BlockSpec((tm, tn), lambda i, j: (i, j)),
        out_shape=jax.ShapeDtypeStruct((m, n), out_dtype),
        compiler_params=_params("parallel", "parallel"),
        name="matmul",
    )(h, w)


def _ctx_attn_kernel(q_ref, k_ref, v_ref, z_ref, o_ref, *, heads):
    scale = HEAD_DIM ** -0.5
    for h in range(heads):
        sl = slice(h * HEAD_DIM, (h + 1) * HEAD_DIM)
        q = q_ref[:, sl]
        k = k_ref[:, sl].astype(BF16)
        v = v_ref[:, sl].astype(BF16)
        s = lax.dot_general(q, k, NT_DIMS, preferred_element_type=F32) * scale
        p = jnp.exp(s - jnp.max(s, axis=-1, keepdims=True))
        o = jnp.dot(p.astype(BF16), v, preferred_element_type=F32) / jnp.sum(p, axis=-1, keepdims=True)
        o_ref[:, sl] = (o * _silu(z_ref[:, sl].astype(F32))).astype(BF16)


def context_attention(q, k, v, z, *, seq_len):
    m, width = q.shape
    spec = pl.BlockSpec((seq_len, width), lambda b: (b, 0))
    return pl.pallas_call(
        functools.partial(_ctx_attn_kernel, heads=width // HEAD_DIM),
        grid=(m // seq_len,),
        in_specs=[spec, spec, spec, spec],
        out_specs=spec,
        out_shape=jax.ShapeDtypeStruct((m, width), BF16),
        compiler_params=_params("parallel"),
        name="context_attention",
    )(q, k, v, z)


ATTN_Q_ROWS = 4
ATTN_GROUP = 4


def _attn_plan(rows):
    wr = min(WIN_R, rows)
    qb = min(ATTN_Q_ROWS, rows)
    kw = min(qb + wr - 1 + (qb + wr - 1) % 2, rows)
    assert rows % qb == 0 and kw % 2 == 0

    def window_start(r):
        return min(max(r - wr // 2, 0), rows - wr)

    blocks, patterns = [], []
    for r0 in range(0, rows, qb):
        k0 = min(max(r0 - wr // 2, 0), rows - kw)
        pat = tuple(tuple((k0 + i) - (r0 + j) + WIN_R - 1
                          if window_start(r0 + j) <= k0 + i < window_start(r0 + j) + wr else None
                          for i in range(kw)) for j in range(qb))
        assert all(sum(d is not None for d in row) == wr for row in pat)
        if pat not in patterns:
            patterns.append(pat)
        blocks.append((r0, k0, patterns.index(pat)))
    return qb, kw, blocks, patterns


def _bias_kernel(rpb_ref, o_ref, *, rows):
    h = pl.program_id(0)
    qb, kw, _, patterns = _attn_plan(rows)
    n_dr, n_dc = 2 * WIN_R - 1, 2 * WIN_C - 1
    lane = lax.broadcasted_iota(jnp.int32, (GRID_W, 2 * GRID_W), 1)
    qi = lax.broadcasted_iota(jnp.int32, (GRID_W, 2 * GRID_W), 0)
    kcol = lane & (GRID_W - 1)
    second = lane >= GRID_W
    dc = kcol - qi + (WIN_C - 1)
    col_start = jnp.clip(qi - WIN_C // 2, 0, GRID_W - WIN_C)
    col_in = (kcol >= col_start) & (kcol < col_start + WIN_C)
    cache = {}

    def tile(d_lo, d_hi):
        if (d_lo, d_hi) not in cache:
            acc = jnp.zeros((GRID_W, 2 * GRID_W), F32)
            ok = col_in
            if d_lo is None:
                ok = ok & second
            if d_hi is None:
                ok = ok & jnp.logical_not(second)
            if d_lo is not None or d_hi is not None:
                for c in range(n_dc):
                    lo = rpb_ref[(h * n_dr + d_lo) * n_dc + c] if d_lo is not None else 0.0
                    hi = rpb_ref[(h * n_dr + d_hi) * n_dc + c] if d_hi is not None else 0.0
                    acc = jnp.where(dc == c, jnp.where(second, hi, lo), acc)
            cache[(d_lo, d_hi)] = jnp.where(ok, acc, -jnp.inf)
        return cache[(d_lo, d_hi)]

    n_ctx = o_ref.shape[3] - kw * GRID_W
    for t, pat in enumerate(patterns):
        for j in range(qb):
            for i in range(0, kw, 2):
                o_ref[0, t, j * GRID_W:(j + 1) * GRID_W, i * GRID_W:(i + 2) * GRID_W] = tile(pat[j][i], pat[j][i + 1])
        o_ref[0, t, :, kw * GRID_W:] = jnp.zeros((qb * GRID_W, n_ctx), F32)


def window_bias(rpb, *, rows, ctx_len):
    heads = rpb.shape[0]
    qb, kw, _, patterns = _attn_plan(rows)
    shape = (len(patterns), qb * GRID_W, kw * GRID_W + ctx_len)
    return pl.pallas_call(
        functools.partial(_bias_kernel, rows=rows),
        grid=(heads,),
        in_specs=[pl.BlockSpec(memory_space=pltpu.SMEM)],
        out_specs=pl.BlockSpec((1,) + shape, lambda h: (h, 0, 0, 0)),
        out_shape=jax.ShapeDtypeStruct((heads,) + shape, F32),
        compiler_params=_params("parallel"),
        name="window_bias",
    )(rpb.reshape(-1))


def _lat_attn_kernel(q_ref, k_ref, v_ref, z_ref, ck_ref, cv_ref, tbl_ref, o_ref, *, rows):
    scale = HEAD_DIM ** -0.5
    qb, kw, blocks, _ = _attn_plan(rows)
    ck = ck_ref[0].astype(BF16)
    cv = cv_ref[0].astype(BF16)
    for g in range(0, len(blocks), ATTN_GROUP):
        grp = blocks[g:g + ATTN_GROUP]
        qrows = [slice(r0 * GRID_W, (r0 + qb) * GRID_W) for r0, _, _ in grp]
        krows = [slice(k0 * GRID_W, (k0 + kw) * GRID_W) for _, k0, _ in grp]
        s = [lax.dot_general(q_ref[qr, :], jnp.concatenate([k_ref[kr, :], ck], axis=0), NT_DIMS,
                             preferred_element_type=F32) * scale + tbl_ref[0, t]
             for qr, kr, (_, _, t) in zip(qrows, krows, grp)]
        p = [jnp.exp(x - jnp.max(x, axis=-1, keepdims=True)) for x in s]
        o = [jnp.dot(x.astype(BF16), jnp.concatenate([v_ref[kr, :], cv], axis=0), preferred_element_type=F32)
             / jnp.sum(x, axis=-1, keepdims=True) for x, kr in zip(p, krows)]
        for x, qr in zip(o, qrows):
            o_ref[qr, :] = (x * _silu(z_ref[qr, :].astype(F32))).astype(BF16)


def neighbourhood_attention(qkvz, ck, cv, bias, *, seq_len):
    m = qkvz.shape[0]
    heads = qkvz.shape[1] // (4 * HEAD_DIM)
    rows = seq_len // GRID_W
    ctx_len = ck.shape[1]
    tok = [pl.BlockSpec((seq_len, HEAD_DIM), functools.partial(lambda h, b, part: (b, part * heads + h), part=part))
           for part in range(4)]
    ctx = pl.BlockSpec((1, ctx_len, HEAD_DIM), lambda h, b: (b, 0, h))
    return pl.pallas_call(
        functools.partial(_lat_attn_kernel, rows=rows),
        grid=(heads, m // seq_len),
        in_specs=tok + [ctx, ctx, pl.BlockSpec((1,) + bias.shape[1:], lambda h, b: (h, 0, 0, 0))],
        out_specs=tok[0],
        out_shape=jax.ShapeDtypeStruct((m, heads * HEAD_DIM), BF16),
        compiler_params=_params("parallel", "parallel"),
        name="neighbourhood_attention",
    )(qkvz, qkvz, qkvz, qkvz, ck, cv, bias)


CONV_PAD = 16
CONV_TILE_ELEMS = 1 << 18


def _conv_kernel(x_ref, w_ref, o_ref, pad_ref, *, l2_scale, rows_per_step):
    t = x_ref.shape[1]
    tc = x_ref.shape[2]
    rps = rows_per_step
    half = DN_CONV // 2
    win = rps + 2 * CONV_PAD
    pad_ref[0:CONV_PAD, :] = jnp.zeros((CONV_PAD, tc), BF16)
    pad_ref[CONV_PAD + t:2 * CONV_PAD + t, :] = jnp.zeros((CONV_PAD, tc), BF16)
    pad_ref[CONV_PAD:CONV_PAD + t, :] = x_ref[0]

    taps = [j for j in range(DN_CONV) if j != half]
    r = lax.broadcasted_iota(jnp.int32, (len(taps) * rps, win), 0)
    c = lax.broadcasted_iota(jnp.int32, (len(taps) * rps, win), 1)
    src = (r & (rps - 1)) + CONV_PAD - half
    for m, j in enumerate(taps):
        src = src + jnp.where(_div_pow2(r, rps) == m, j, 0)
    shift = jnp.where(c == src, 1.0, 0.0).astype(BF16)

    for r0 in range(0, t, rps):
        xw = pad_ref[r0:r0 + win, :]
        moved = jnp.dot(shift, xw, preferred_element_type=F32)
        acc = xw[CONV_PAD:CONV_PAD + rps].astype(F32) * w_ref[half:half + 1, :]
        for m, j in enumerate(taps):
            acc = acc + moved[m * rps:(m + 1) * rps] * w_ref[j:j + 1, :]
        y = _silu(acc)
        if l2_scale is not None:
            parts = []
            for g in range(tc // HEAD_DIM):
                yg = y[:, g * HEAD_DIM:(g + 1) * HEAD_DIM]
                parts.append(yg * (lax.rsqrt(jnp.sum(yg * yg, axis=-1, keepdims=True) + EPS) * l2_scale))
            y = jnp.concatenate(parts, axis=1)
        o_ref[0, r0:r0 + rows_per_step, :] = y.astype(BF16)


def conv_silu(x, conv_w, *, seq_len, col0, width, l2_scale, tc=512, rows_per_step=128):
    m, c_total = x.shape
    b = m // seq_len
    rows_per_step = min(rows_per_step, seq_len)
    tc = min(width, max(tc, CONV_TILE_ELEMS // seq_len))
    assert width % tc == 0 and col0 % tc == 0
    out = pl.pallas_call(
        functools.partial(_conv_kernel, l2_scale=l2_scale, rows_per_step=rows_per_step),
        grid=(b, width // tc),
        in_specs=[pl.BlockSpec((1, seq_len, tc), lambda i, j: (i, 0, col0 // tc + j)),
                  pl.BlockSpec((DN_CONV, tc), lambda i, j: (0, col0 // tc + j))],
        out_specs=pl.BlockSpec((1, seq_len, tc), lambda i, j: (i, 0, j)),
        out_shape=jax.ShapeDtypeStruct((b, seq_len, width), BF16),
        scratch_shapes=[pltpu.VMEM((seq_len + 2 * CONV_PAD, tc), BF16)],
        compiler_params=_params("parallel", "parallel"),
        name="conv_silu",
    )(x.reshape(b, seq_len, c_total), conv_w)
    return out.reshape(m, width)


def _gate_kernel(ba_ref, alog_ref, dtb_ref, o_ref):
    x = ba_ref[...]
    tm = x.shape[0]
    lane = lax.broadcasted_iota(jnp.int32, x.shape, 1)
    beta = jax.nn.sigmoid(x)
    xs = x + dtb_ref[...]
    softplus = jnp.maximum(xs, 0.0) + jnp.log1p(jnp.exp(-jnp.abs(xs)))
    g = jnp.where(lane >= 2 * DN_V_HEADS, -jnp.exp(alog_ref[...]) * softplus, 0.0)
    r = lax.broadcasted_iota(jnp.int32, (tm, tm), 0)
    c = lax.broadcasted_iota(jnp.int32, (tm, tm), 1)
    same = _div_pow2(r, DN_CHUNK) == _div_pow2(c, DN_CHUNK)
    lower = jnp.where(same & (c <= r), 1.0, 0.0).astype(BF16)
    upper = jnp.where(same & (c >= r), 1.0, 0.0).astype(BF16)
    dot = functools.partial(jnp.dot, preferred_element_type=F32)
    hi, mid, lo = _split3(g)
    cf = dot(lower, hi) + dot(lower, mid) + dot(lower, lo)
    cb = dot(upper, hi) + dot(upper, mid) + dot(upper, lo)
    gc = jnp.where(lane < 3 * DN_V_HEADS, cf, cb)
    o_ref[...] = jnp.where(lane < 2 * DN_V_HEADS, beta, gc)


def gate_sums(ba, a_log, dt_bias, *, tm=256):
    m, w = ba.shape
    zeros = jnp.zeros((2 * DN_V_HEADS,), F32)
    alog = jnp.concatenate([zeros, a_log.reshape(-1)]).reshape(1, w)
    dtb = jnp.concatenate([zeros, dt_bias.reshape(-1)]).reshape(1, w)
    return pl.pallas_call(
        _gate_kernel,
        grid=(m // tm,),
        in_specs=[pl.BlockSpec((tm, w), lambda i: (i, 0)),
                  pl.BlockSpec((1, w), lambda i: (0, 0)),
                  pl.BlockSpec((1, w), lambda i: (0, 0))],
        out_specs=pl.BlockSpec((tm, w), lambda i: (i, 0)),
        out_shape=jax.ShapeDtypeStruct((m, w), F32),
        compiler_params=_params("parallel"),
        name="gate_sums",
    )(ba, alog, dtb)


N_PROB = 4
PREPARE_GROUP = 8


def _block_diag(x, mask):
    xb = x.astype(BF16)
    return jnp.where(mask, jnp.concatenate([xb] * N_PROB, axis=0), jnp.zeros((), BF16))


def _unit_tri_inverse(a_list, row, col, eye, bd_mask):
    cs = a_list[0].shape[0]

    def same_block(b):
        return _div_pow2(row, b) == _div_pow2(col, b)

    def mm(xs, ys_bd):
        return [jnp.dot(x.astype(BF16), y, preferred_element_type=F32) for x, y in zip(xs, ys_bd)]

    def bd(xs):
        return [_block_diag(x, bd_mask) for x in xs]

    base = 8
    d = [jnp.where(same_block(base), a, 0.0) for a in a_list]
    d2 = mm(d, bd(d))
    d2_bd = bd(d2)
    d4_bd = bd(mm(d2, d2_bd))
    t = [eye - x for x in d]
    t = [x + y for x, y in zip(t, mm(t, d2_bd))]
    t = [x + y for x, y in zip(t, mm(t, d4_bd))]
    b = base
    while b < cs:
        join = same_block(2 * b) & jnp.logical_not(same_block(b))
        e_bd = bd([jnp.where(join, a, 0.0) for a in a_list])
        t = [x - y for x, y in zip(t, mm(mm(t, e_bd), bd(t)))]
        b *= 2
    return t


def _delta_kernel(*refs, n_chunks, hb, has_init, emit_state):
    q_ref, k_ref, v_ref, z_ref, pc_ref, pr_ref, nw_ref = refs[:7]
    refs = refs[7:]
    s0_ref = None
    if has_init:
        s0_ref, refs = refs[0], refs[1:]
    o_ref, refs = refs[0], refs[1:]
    sn_ref = None
    if emit_state:
        sn_ref, refs = refs[0], refs[1:]
    acc_ref, s_ref, n_ref, pm_ref = refs
    cs, hd = DN_CHUNK, HEAD_DIM
    wide = N_PROB * cs

    row = lax.broadcasted_iota(jnp.int32, (cs, wide), 0)
    lane_d = lax.broadcasted_iota(jnp.int32, (hd, wide), 1)
    lane = lax.broadcasted_iota(jnp.int32, (cs, wide), 1)
    col = lane & (cs - 1)
    fwd = lane < 2 * cs
    strict = (fwd & (col < row)) | (jnp.logical_not(fwd) & (col > row))
    incl = strict | (col == row)
    eye = jnp.where(col == row, 1.0, 0.0)
    first_half = (lax.broadcasted_iota(jnp.int32, (cs, 2 * cs), 1) < cs)
    bd_mask = (_div_pow2(lax.broadcasted_iota(jnp.int32, (wide, wide), 0), cs)
               == _div_pow2(lax.broadcasted_iota(jnp.int32, (wide, wide), 1), cs))

    def side_by_side(c0, c1, c2, c3):
        half = first_half[:c0.shape[0]]
        return jnp.concatenate([jnp.where(half, c0, c1), jnp.where(half, c2, c3)], axis=1)

    group = min(max(PREPARE_GROUP // hb, 1), n_chunks)
    units = [(cj, hh) for cj in range(group) for hh in range(hb)]
    each = range(len(units))
    head = [hh for _, hh in units]

    def prepare(gi, carry):
        cidx = [gi * group + cj for cj, _ in units]
        rows = [pl.ds(pl.multiple_of(c * cs, cs), cs) for c in cidx]
        qc = [q_ref[r, hh * hd:(hh + 1) * hd] for r, hh in zip(rows, head)]
        kc = [k_ref[r, hh * hd:(hh + 1) * hd] for r, hh in zip(rows, head)]
        kf = [x.astype(F32) for x in kc]
        pc = [pc_ref[0, hh, r, :] for r, hh in zip(rows, head)]
        gcr = [pr_ref[0, hh, c] for c, hh in zip(cidx, head)]
        beta = [[x[:, p:p + 1] for p in range(N_PROB)] for x in pc]
        gcc = [[x[:, N_PROB + p:N_PROB + p + 1] for p in range(N_PROB)] for x in pc]
        prod = [lax.dot_general(jnp.concatenate([qc[j], kc[j]], axis=0), jnp.concatenate([kc[j]] * N_PROB, axis=0),
                                NT_DIMS, preferred_element_type=F32) for j in each]
        decay = [jnp.exp(jnp.where(incl, side_by_side(*gcc[j]) - gcr[j], -jnp.inf)) for j in each]
        a = [jnp.where(strict, side_by_side(*beta[j]) * prod[j][cs:] * decay[j], 0.0) for j in each]
        t = _unit_tri_inverse(a, row, col, eye, bd_mask)
        rhs = []
        for j in each:
            parts = []
            for p in range(N_PROB):
                vcol = (2 * head[j] + p % 2) * hd
                vf = v_ref[rows[j], vcol:vcol + hd].astype(F32)
                parts.append(jnp.concatenate([vf * beta[j][p], kf[j] * (beta[j][p] * jnp.exp(gcc[j][p]))], axis=1))
            rhs.append(jnp.concatenate(parts, axis=0).astype(BF16))
        sol = [jnp.dot(_block_diag(t[j], bd_mask), rhs[j], preferred_element_type=F32).astype(BF16)
               for j in each]
        osol = [jnp.dot(_block_diag(prod[j][:cs] * decay[j], bd_mask), sol[j], preferred_element_type=F32)
                for j in each]
        ksol = []
        for j in each:
            g_last = [gcc[j][p][(cs - 1 if p < 2 else 0):(cs if p < 2 else 1), :] for p in range(N_PROB)]
            kdt = jnp.concatenate([kf[j]] * N_PROB, axis=0).T * jnp.exp(side_by_side(*g_last) - gcr[j])
            lhs = jnp.concatenate([jnp.where((lane_d >= p * cs) & (lane_d < (p + 1) * cs), kdt, 0.0)
                                   for p in range(N_PROB)], axis=0).astype(BF16)
            ksol.append(jnp.dot(lhs, sol[j], preferred_element_type=F32))
        for j in each:
            qf = qc[j].astype(F32)
            for p in range(N_PROB):
                kp = ksol[j][p * hd:(p + 1) * hd]
                op = osol[j][p * cs:(p + 1) * cs]
                n_ref[cidx[j], head[j] * N_PROB + p] = kp[:, :hd]
                pm_ref[cidx[j], head[j] * N_PROB + p, 0:hd] = (-kp[:, hd:]).astype(BF16)
                pm_ref[cidx[j], head[j] * N_PROB + p, hd:hd + cs] = (
                    qf * jnp.exp(gcc[j][p]) - op[:, hd:]).astype(BF16)
            for e in range(2):
                ocol = (2 * head[j] + e) * hd
                acc_ref[rows[j], ocol:ocol + hd] = (osol[j][e * cs:(e + 1) * cs, :hd]
                                                    + osol[j][(2 + e) * cs:(3 + e) * cs, :hd])
        return carry

    lax.fori_loop(0, n_chunks // group, prepare, 0)

    problems = [(hh, p) for hh in range(hb) for p in range(N_PROB)]
    for hh, p in problems:
        s_ref[hh * N_PROB + p] = s0_ref[0, p // 2, 2 * hh + p % 2] if has_init else jnp.zeros((hd, hd), F32)

    def advance(ci, carry):
        for hh, p in problems:
            ocol = (2 * hh + p % 2) * hd
            c = ci if p < 2 else n_chunks - 1 - ci
            last = cs - 1 if p < 2 else 0
            rows = pl.ds(pl.multiple_of(c * cs, cs), cs)
            g_last = pc_ref[0, hh, pl.ds(c * cs + last, 1), N_PROB + p:N_PROB + p + 1]
            s = s_ref[hh * N_PROB + p]
            ws = jnp.dot(pm_ref[c, hh * N_PROB + p], s.astype(BF16),
                         preferred_element_type=F32)
            s_ref[hh * N_PROB + p] = s * jnp.exp(g_last) + n_ref[c, hh * N_PROB + p] + ws[:hd]
            acc_ref[rows, ocol:ocol + hd] += ws[hd:]
        return carry

    lax.fori_loop(0, n_chunks, advance, 0)

    if emit_state:
        for hh, p in problems:
            sn_ref[0, 0, p // 2, 2 * hh + p % 2] = s_ref[hh * N_PROB + p]
    for e in range(2 * hb):
        sl = slice(e * HEAD_DIM, (e + 1) * HEAD_DIM)
        o = acc_ref[:, sl]
        y = o * lax.rsqrt(jnp.mean(o * o, axis=-1, keepdims=True) + EPS) * nw_ref[...]
        o_ref[:, sl] = (y * _silu(z_ref[:, sl].astype(F32))).astype(BF16)


def gated_delta(q, k, v, z, z_col0, p, norm_w, s0, *, seq_len, emit_state):
    m = q.shape[0]
    b = m // seq_len
    n_chunks = seq_len // DN_CHUNK
    rep = DN_V_HEADS // DN_K_HEADS
    assert 2 * rep == N_PROB
    p5 = p.reshape(b, seq_len, 2, 2, DN_K_HEADS, rep)
    pc = p5.transpose(0, 4, 1, 2, 3, 5).reshape(b, DN_K_HEADS, seq_len, 2 * N_PROB)
    pr = p5[:, :, 1].reshape(b, n_chunks, DN_CHUNK, 2, DN_K_HEADS, rep).transpose(0, 4, 1, 3, 5, 2).reshape(
        b, DN_K_HEADS, n_chunks, 1, N_PROB * DN_CHUNK)
    d = HEAD_DIM
    per_head = (n_chunks * N_PROB * (d * d * 4 + (d + DN_CHUNK) * d * 2) + seq_len * rep * d * 4
                + 2 * seq_len * (2 * d * 2 + 3 * rep * d * 2 + HEAD_DIM * 4))
    hb = max(h for h in (1, 2, 4, 8) if h == 1 or h * per_head <= VMEM_LIMIT // 2)
    z_block0 = z_col0 // (hb * rep * d)
    assert z_block0 * hb * rep * d == z_col0
    in_specs = [pl.BlockSpec((seq_len, hb * d), lambda i, h: (i, h)),
                pl.BlockSpec((seq_len, hb * d), lambda i, h: (i, h)),
                pl.BlockSpec((seq_len, hb * rep * d), lambda i, h: (i, h)),
                pl.BlockSpec((seq_len, hb * rep * d), lambda i, h: (i, z_block0 + h)),
                pl.BlockSpec((1, hb, seq_len, 2 * N_PROB), lambda i, h: (i, h, 0, 0)),
                pl.BlockSpec((1, hb, n_chunks, 1, N_PROB * DN_CHUNK), lambda i, h: (i, h, 0, 0, 0)),
                pl.BlockSpec((1, d), lambda i, h: (0, 0))]
    args = [q, k, v, z, pc, pr, norm_w.reshape(1, d)]
    if s0 is not None:
        in_specs.append(pl.BlockSpec((1, 2, hb * rep, d, d), lambda i, h: (i, 0, h, 0, 0)))
        args.append(s0)
    out_specs = [pl.BlockSpec((seq_len, hb * rep * d), lambda i, h: (i, h))]
    out_shape = [jax.ShapeDtypeStruct((m, DN_V_HEADS * d), BF16)]
    if emit_state:
        out_specs.append(pl.BlockSpec((1, 1, 2, hb * rep, d, d), lambda i, h: (i, 0, 0, h, 0, 0)))
        out_shape.append(jax.ShapeDtypeStruct((b, 1, 2, DN_V_HEADS, d, d), F32))
    res = pl.pallas_call(
        functools.partial(_delta_kernel, n_chunks=n_chunks, hb=hb, has_init=s0 is not None,
                          emit_state=emit_state),
        grid=(b, DN_K_HEADS // hb),
        in_specs=in_specs,
        out_specs=out_specs,
        out_shape=out_shape,
        scratch_shapes=[pltpu.VMEM((seq_len, hb * rep * d), F32),
                        pltpu.VMEM((hb * N_PROB, d, d), F32),
                        pltpu.VMEM((n_chunks, hb * N_PROB, d, d), F32),
                        pltpu.VMEM((n_chunks, hb * N_PROB, d + DN_CHUNK, d), BF16)],
        compiler_params=_params("parallel", "parallel"),
        name="gated_delta",
    )(*args)
    return res if emit_state else (res[0], None)


def _trunk(x, mod, rows, cache_k, cache_v, state, weights, *, emit):
    (norm_w, na_w_in, na_w_out, bias, dn_w_in, dn_w_ba, dn_conv_w, dn_a_log, dn_dt_bias,
     dn_norm_w, dn_w_out, final_norm_w) = weights
    b, t, d = x.shape
    x2 = x.reshape(b * t, d)
    kw = dict(seq_len=t)

    def mods(layer):
        mm = mod[layer, rows][:, None, :]
        return mm[..., :d], mm[..., d:2 * d], mm[..., 2 * d:]

    shift, scale, gate = mods(0)
    h0 = norm_modulate(x2, norm_w[0], shift, scale, **kw)
    if cache_k is not None:
        k = v = None
        g = neighbourhood_attention(matmul(h0, na_w_in, BF16), cache_k, cache_v, bias, **kw)
    else:
        q = matmul(h0, na_w_in, BF16, col0=0, width=d)
        k = matmul(h0, na_w_in, F32, col0=d, width=d)
        v = matmul(h0, na_w_in, F32, col0=2 * d, width=d)
        z = matmul(h0, na_w_in, BF16, col0=3 * d, width=d)
        g = context_attention(q, k, v, z, **kw)
    shift, scale, gate1 = mods(1)
    x2, h1 = out_residual(g, na_w_out, x2, gate, norm_w[1], (shift, scale), tm=512, **kw)

    kw_dim, vw_dim = DN_K_HEADS * HEAD_DIM, DN_V_HEADS * HEAD_DIM
    proj = matmul(h1, dn_w_in, BF16)
    ba = matmul(h1, dn_w_ba, F32)
    qn = conv_silu(proj, dn_conv_w, col0=0, width=kw_dim, l2_scale=HEAD_DIM ** -0.5, **kw)
    kn = conv_silu(proj, dn_conv_w, col0=kw_dim, width=kw_dim, l2_scale=1.0, **kw)
    vn = conv_silu(proj, dn_conv_w, col0=2 * kw_dim, width=vw_dim, l2_scale=None, **kw)
    p = gate_sums(ba, dn_a_log, dn_dt_bias)
    og, s_new = gated_delta(qn, kn, vn, proj, 2 * kw_dim + vw_dim, p, dn_norm_w, state, emit_state=emit, **kw)
    y = out_residual(og, dn_w_out, x2, gate1, final_norm_w, None, tm=256, **kw)
    return y.reshape(b, t, d), k, v, s_new


def kernel(x_prompt, x_sample, cache_na_k, cache_na_v, state_dn, c, c_ctx, norm_w, ada_w, ada_b, na_w_in, na_w_out, na_rpb, dn_w_in, dn_conv_w, dn_a_log, dn_dt_bias, dn_norm_w, dn_w_out, final_norm_w):
    bp, tp, d = x_prompt.shape
    bs, ts, _ = x_sample.shape
    ctx_len = cache_na_k.shape[2]
    assert norm_w.shape[0] == 2 and na_w_in.shape[0] == 1 and dn_w_in.shape[0] == 1

    n_cond = -(-(bs + 1) // SUBLANES) * SUBLANES
    cond = jnp.concatenate([c, c_ctx[None], jnp.zeros((n_cond - bs - 1, d), F32)])
    mod = ada_mod(cond, ada_w, ada_b)

    conv_width = 2 * DN_K_HEADS * HEAD_DIM + DN_V_HEADS * HEAD_DIM
    n_main = conv_width + DN_V_HEADS * HEAD_DIM
    weights = (norm_w, na_w_in[0].astype(BF16), na_w_out[0].astype(BF16),
               window_bias(na_rpb[0], rows=ts // GRID_W, ctx_len=ctx_len),
               dn_w_in[0, :, :n_main].astype(BF16), dn_w_in[0, :, n_main:].astype(BF16),
               dn_conv_w[0], dn_a_log[0], dn_dt_bias[0], dn_norm_w[0], dn_w_out[0].astype(BF16),
               final_norm_w)

    y_prompt, k_new, v_new, s_new = _trunk(
        x_prompt, mod, slice(bs, bs + 1), None, None, None, weights, emit=True)
    y_sample, _, _, _ = _trunk(
        x_sample, mod, slice(0, bs), cache_na_k[:, 0].reshape(bs, ctx_len, d),
        cache_na_v[:, 0].reshape(bs, ctx_len, d), state_dn[:, 0], weights, emit=False)

    kv_shape = (bp, 1, tp, NA_HEADS, HEAD_DIM)
    return (y_prompt, y_sample, k_new.reshape(kv_shape), v_new.reshape(kv_shape), s_new)
```
